```python
import jax, jax.numpy as jnp
from jax import lax
import numpy as np

D_MODEL = 1024
BATCH = 2
SEQ = 8192
DEPTH = 1
DEC_BATCH = 128
DEC_SEQ = 4
PAST_LEN = 8192
PAGE_SIZE = 128

M_HEADS = 4
M_DQK = 128
M_DV = D_MODEL // M_HEADS
M_CONV = 4
M_CHUNK = 64
N_HEADS = 16
N_KV = 4
N_GROUP = N_HEADS // N_KV
N_HD = D_MODEL // N_HEADS
CMP_BLOCK = 32
CMP_HIDDEN = 2 * N_HD
SLC_BLOCK = 64
SLC_TOPK = 16
WINDOW = 512
Q_BLOCK = 128
ROPE_THETA = 500000.0
ROPE_DIM = N_HD // 4
P_HEADS = 8
P_NKEYS = 128
P_EXPERTS = P_NKEYS * P_NKEYS
P_DKEY = 256
P_TOPK = 16
P_BLOCK = 128

EPS = 1e-6
F32 = jnp.float32

IN_LAYOUT = (
    ('m_qk', 2 * M_HEADS * M_DQK),
    ('m_v', M_HEADS * M_DV),
    ('m_i', M_HEADS),
    ('m_f', M_HEADS),
    ('m_o', M_HEADS * M_DV),
    ('n_q', N_HEADS * N_HD),
    ('n_kc', N_KV * N_HD),
    ('n_vc', N_KV * N_HD),
    ('n_ks', N_KV * N_HD),
    ('n_vs', N_KV * N_HD),
    ('n_kw', N_KV * N_HD),
    ('n_vw', N_KV * N_HD),
    ('n_g', 3 * N_HEADS),
    ('g_a', D_MODEL),
    ('g_b', D_MODEL),
)

kernel_name = 'hybrid_mlstm_nsa_peer_decode_step'


def rmsnorm(x, g):
    xf = x.astype(F32)
    r = xf * lax.rsqrt(jnp.mean(xf * xf, axis=-1, keepdims=True) + EPS)
    return (r * g.astype(F32)).astype(x.dtype)


def split_columns(z):
    out, off = {}, 0
    for name, width in IN_LAYOUT:
        out[name] = z[..., off:off + width]
        off += width
    return out


def partial_rope(x, pos):
    half = ROPE_DIM // 2
    inv = ROPE_THETA ** (-jnp.arange(half, dtype=F32) / half)
    ang = pos.astype(F32)[:, None] * inv
    cos, sin = jnp.cos(ang)[:, None, :], jnp.sin(ang)[:, None, :]
    x1 = x[..., :half].astype(F32)
    x2 = x[..., half:ROPE_DIM].astype(F32)
    rot = jnp.concatenate([x1 * cos - x2 * sin, x2 * cos + x1 * sin], axis=-1)
    return jnp.concatenate([rot.astype(x.dtype), x[..., ROPE_DIM:]], axis=-1)


def masked_softmax(s, mask):
    s = jnp.where(mask, s.astype(F32), -jnp.inf)
    m = jnp.max(s, axis=-1, keepdims=True)
    m = jnp.where(jnp.isfinite(m), m, 0.0)
    e = jnp.where(mask, jnp.exp(s - m), 0.0)
    return e / jnp.maximum(e.sum(axis=-1, keepdims=True), 1.0)


def mlstm_chunkwise(q, k, v, log_i, log_f, C0, n0, m0):
    B, T, H, _ = q.shape
    L = M_CHUNK if T % M_CHUNK == 0 else T
    nc = T // L
    chunks = lambda a: jnp.moveaxis(a.reshape((B, nc, L) + a.shape[2:]), 1, 0)
    causal = jnp.tril(jnp.ones((L, L), dtype=bool))[None, :, :, None]

    def step(carry, inp):
        C, n, m = carry
        qc, kc, vc, li, lf = inp
        b = jnp.cumsum(lf, axis=1)
        logw = jnp.where(causal, b[:, :, None] - b[:, None, :] + li[:, None, :], -jnp.inf)
        inter = b + m[:, None]
        m_t = jnp.maximum(inter, logw.max(axis=2))
        w = jnp.exp(logw - m_t[:, :, None]) * jnp.einsum('bthd,bshd->btsh', qc, kc)
        w_inter = jnp.exp(inter - m_t)
        num = jnp.einsum('btsh,bshv->bthv', w, vc) + w_inter[..., None] * jnp.einsum('bthd,bhdv->bthv', qc, C)
        den = w.sum(axis=2) + w_inter * jnp.einsum('bthd,bhd->bth', qc, n)
        h = num / jnp.maximum(jnp.abs(den), jnp.exp(-m_t))[..., None]
        m_new = m_t[:, -1]
        w_end = jnp.exp(b[:, -1:] - b + li - m_new[:, None])
        decay = jnp.exp(b[:, -1] + m - m_new)
        C_new = decay[..., None, None] * C + jnp.einsum('bsh,bshd,bshv->bhdv', w_end, kc, vc)
        n_new = decay[..., None] * n + jnp.einsum('bsh,bshd->bhd', w_end, kc)
        return (C_new, n_new, m_new), h

    xs = (chunks(q), chunks(k), chunks(v), chunks(log_i), chunks(log_f))
    (C, n, m), hs = lax.scan(step, (C0, n0, m0), xs)
    return jnp.moveaxis(hs, 0, 1).reshape(B, T, H, -1), C, n, m


def mlstm_branch(cols, conv_buf, C0, n0, m0, p):
    qk_in = cols['m_qk']
    B, T, W = qk_in.shape
    ext = jnp.concatenate([conv_buf.astype(qk_in.dtype), qk_in], axis=1)
    conv = p['m_conv_b'].astype(F32) + ext[:, 0:T].astype(F32) * p['m_conv_w'][0]
    for j in range(1, M_CONV):
        conv = conv + ext[:, j:j + T].astype(F32) * p['m_conv_w'][j]
    qk = jax.nn.silu(conv)
    q = qk[..., :W // 2].reshape(B, T, M_HEADS, M_DQK)
    k = qk[..., W // 2:].reshape(B, T, M_HEADS, M_DQK) * (M_DQK ** -0.5)
    v = cols['m_v'].astype(F32).reshape(B, T, M_HEADS, M_DV)
    log_i = cols['m_i'].astype(F32) + p['m_bias_i'].astype(F32)
    log_f = jax.nn.log_sigmoid(cols['m_f'].astype(F32) + p['m_bias_f'].astype(F32))
    h, C, n, m = mlstm_chunkwise(q, k, v, log_i, log_f, C0.astype(F32), n0.astype(F32), m0.astype(F32))
    h = h * lax.rsqrt(jnp.mean(h * h, axis=-1, keepdims=True) + EPS) * p['m_norm'].astype(F32).reshape(M_HEADS, M_DV)
    h = jax.nn.sigmoid(cols['m_o'].astype(F32)) * h.reshape(B, T, M_HEADS * M_DV)
    return h, ext[:, -(M_CONV - 1):], C.astype(C0.dtype), n.astype(n0.dtype), m.astype(m0.dtype)


def compress(rows, w1, b1, w2):
    B, T = rows.shape[:2]
    blk = rows.reshape(B, T // CMP_BLOCK, CMP_BLOCK, N_KV, N_HD)
    hid = jax.nn.gelu(jnp.einsum('bnjkd,jdh->bnkh', blk, w1).astype(F32) + b1.astype(F32), approximate=False)
    return jnp.einsum('bnkh,hd->bnkd', hid, w2.astype(F32))


def compressed_keys(k_parts, v_parts, p):
    kc = jnp.concatenate([compress(r, p['cmp_k_w1'], p['cmp_k_b1'], p['cmp_k_w2']) for r in k_parts], axis=1)
    vc = jnp.concatenate([compress(r, p['cmp_v_w1'], p['cmp_v_b1'], p['cmp_v_w2']) for r in v_parts], axis=1)
    end_c = jnp.arange(kc.shape[1], dtype=jnp.int32) * CMP_BLOCK + CMP_BLOCK - 1
    return partial_rope(kc, end_c), vc, end_c


def nsa_core(q, tq, kc, vc, end_c, gather_sel, kw, vw, pw, gates):
    B, Q = q.shape[:2]
    scale = N_HD ** -0.5
    s_c = jnp.einsum('bqkgd,bnkd->bqkgn', q, kc).astype(F32) * scale
    mask_c = (end_c[None, :] <= tq[:, None])[None, :, None, None, :]
    p_c = masked_softmax(s_c, mask_c)
    o_cmp = jnp.einsum('bqkgn,bnkd->bqkgd', p_c, vc.astype(F32))
    n_sel = end_c.shape[0] * CMP_BLOCK // SLC_BLOCK
    imp = p_c.sum(axis=3).reshape(B, Q, N_KV, n_sel, SLC_BLOCK // CMP_BLOCK).sum(-1)
    blk = jnp.arange(n_sel, dtype=jnp.int32)
    valid = (blk[None, :] * SLC_BLOCK <= tq[:, None])[None, :, None, :]
    forced = ((blk[None, :] == 0) | (blk[None, :] == (tq // SLC_BLOCK)[:, None]))[None, :, None, :]
    score = jnp.where(valid, jnp.where(forced, jnp.inf, imp), -jnp.inf)
    _, sel = lax.top_k(score, min(SLC_TOPK, n_sel))
    pos_s = (sel[..., None] * SLC_BLOCK + jnp.arange(SLC_BLOCK, dtype=jnp.int32)).reshape(B, Q, N_KV, -1)
    ks, vs = gather_sel(sel)
    ks = ks.reshape(B, Q, N_KV, -1, N_HD)
    vs = vs.reshape(B, Q, N_KV, -1, N_HD)
    s_s = jnp.einsum('bqkgd,bqkjd->bqkgj', q, ks).astype(F32) * scale
    p_s = masked_softmax(s_s, (pos_s <= tq[None, :, None, None])[:, :, :, None, :])
    o_slc = jnp.einsum('bqkgj,bqkjd->bqkgd', p_s, vs.astype(F32))
    s_w = jnp.einsum('bqkgd,bwkd->bqkgw', q, kw).astype(F32) * scale
    dist = tq[:, None] - pw[None, :]
    mask_w = ((dist >= 0) & (dist < WINDOW) & (pw[None, :] >= 0))[None, :, None, None, :]
    p_w = masked_softmax(s_w, mask_w)
    o_win = jnp.einsum('bqkgw,bwkd->bqkgd', p_w, vw.astype(F32))
    return gates[..., 0:1] * o_cmp + gates[..., 1:2] * o_slc + gates[..., 2:3] * o_win


def nsa_prompt(q, kv, gates, p):
    k_cmp, v_cmp, k_slc, v_slc, k_win, v_win = kv
    B, T = q.shape[:2]
    kc, vc, end_c = compressed_keys([k_cmp], [v_cmp], p)
    n_sel = T // SLC_BLOCK
    kb = k_slc.reshape(B, n_sel, SLC_BLOCK, N_KV, N_HD).transpose(0, 3, 1, 2, 4)
    vb = v_slc.reshape(B, n_sel, SLC_BLOCK, N_KV, N_HD).transpose(0, 3, 1, 2, 4)
    bi = jnp.arange(B)[:, None, None, None]
    ki = jnp.arange(N_KV)[None, None, :, None]
    gather_sel = lambda sel: (kb[bi, ki, sel], vb[bi, ki, sel])
    pad = ((0, 0), (WINDOW, 0), (0, 0), (0, 0))
    kw_pad, vw_pad = jnp.pad(k_win, pad), jnp.pad(v_win, pad)

    def one_block(i):
        start = i * Q_BLOCK
        tq = start + jnp.arange(Q_BLOCK, dtype=jnp.int32)
        qb = lax.dynamic_slice_in_dim(q, start, Q_BLOCK, axis=1)
        gb = lax.dynamic_slice_in_dim(gates, start, Q_BLOCK, axis=1)
        kwb = lax.dynamic_slice_in_dim(kw_pad, start, WINDOW + Q_BLOCK, axis=1)
        vwb = lax.dynamic_slice_in_dim(vw_pad, start, WINDOW + Q_BLOCK, axis=1)
        pw = start - WINDOW + jnp.arange(WINDOW + Q_BLOCK, dtype=jnp.int32)
        return nsa_core(qb, tq, kc, vc, end_c, gather_sel, kwb, vwb, pw, gb)

    o = lax.map(one_block, jnp.arange(T // Q_BLOCK, dtype=jnp.int32))
    return jnp.moveaxis(o, 0, 1).reshape(B, T, N_HEADS * N_HD)


def nsa_sample(q, kv, gates, pools, l, page_table, win_k, win_v, p):
    k_cmp, v_cmp, k_slc, v_slc, k_win, v_win = kv
    pool_kc, pool_vc, pool_ks, pool_vs = pools
    B, T = q.shape[:2]
    n_pages = page_table.shape[1]
    past = n_pages * PAGE_SIZE
    t_pad = -(-T // SLC_BLOCK) * SLC_BLOCK
    pad_t = ((0, 0), (0, t_pad - T), (0, 0), (0, 0))
    past_rows = lambda pool: pool[l, page_table].reshape(B, past, N_KV, N_HD)
    kc, vc, end_c = compressed_keys([past_rows(pool_kc), jnp.pad(k_cmp, pad_t)],
                                    [past_rows(pool_vc), jnp.pad(v_cmp, pad_t)], p)
    tq = past + jnp.arange(T, dtype=jnp.int32)
    bi = jnp.arange(B)[:, None, None, None, None]
    ki = jnp.arange(N_KV)[None, None, :, None, None]

    def gather_sel(sel):
        pos = sel[..., None] * SLC_BLOCK + jnp.arange(SLC_BLOCK, dtype=jnp.int32)
        page = page_table[bi, jnp.minimum(pos // PAGE_SIZE, n_pages - 1)]
        off = pos % PAGE_SIZE
        j = jnp.clip(pos - past, 0, T - 1)
        in_past = (pos < past)[..., None]
        pick = lambda pool, new: jnp.where(in_past, pool[l, page, off, ki], new[bi, j, ki])
        return pick(pool_ks, k_slc), pick(pool_vs, v_slc)

    kw = jnp.concatenate([win_k.astype(k_win.dtype), k_win], axis=1)
    vw = jnp.concatenate([win_v.astype(v_win.dtype), v_win], axis=1)
    n_buf = win_k.shape[1]
    pw = jnp.concatenate([past - n_buf + jnp.arange(n_buf, dtype=jnp.int32), tq])
    o = nsa_core(q, tq, kc, vc, end_c, gather_sel, kw, vw, pw, gates)
    return o.reshape(B, T, N_HEADS * N_HD), kw[:, -n_buf:], vw[:, -n_buf:]


def peer(xn, wq, subkeys, u, v):
    n_tok, D = xn.shape
    xpad = jnp.pad(xn, ((0, (-n_tok) % P_BLOCK), (0, 0)))

    def one(xb):
        q = (xb @ wq).astype(F32).reshape(P_BLOCK, P_HEADS, 2, P_DKEY // 2)
        s = jnp.einsum('nhcd,hckd->nhck', q, subkeys.astype(F32))
        sv, si = lax.top_k(s, P_TOPK)
        cand = (sv[:, :, 0, :, None] + sv[:, :, 1, None, :]).reshape(P_BLOCK, P_HEADS, -1)
        cidx = (si[:, :, 0, :, None] * P_NKEYS + si[:, :, 1, None, :]).reshape(P_BLOCK, P_HEADS, -1)
        best, j = lax.top_k(cand, P_TOPK)
        eidx = jnp.take_along_axis(cidx, j, axis=-1)
        g = jax.nn.softmax(best, axis=-1)
        act = jax.nn.gelu(jnp.einsum('nd,nhkd->nhk', xb, u[eidx]).astype(F32), approximate=False)
        out = jnp.einsum('nhk,nhkd->nd', g * act, v[eidx].astype(F32))
        return out.astype(xb.dtype)

    out = lax.map(one, xpad.reshape(-1, P_BLOCK, D))
    return out.reshape(-1, D)[:n_tok]


def mixer_inputs(x, pos, p):
    B, T, _ = x.shape
    cols = split_columns(rmsnorm(x, p['norm_mix']) @ p['w_in'])
    q = partial_rope(cols['n_q'].reshape(B, T, N_HEADS, N_HD), pos).reshape(B, T, N_KV, N_GROUP, N_HD)
    rows = lambda name: cols[name].reshape(B, T, N_KV, N_HD)
    kv = (rows('n_kc'), rows('n_vc'), partial_rope(rows('n_ks'), pos), rows('n_vs'),
          partial_rope(rows('n_kw'), pos), rows('n_vw'))
    gates = jax.nn.sigmoid(cols['n_g'].astype(F32)).reshape(B, T, N_KV, N_GROUP, 3)
    return cols, q, kv, gates


def merge_and_channel_mix(x, cols, h_a, h_b, p):
    B, T, D = x.shape
    g_a = jax.nn.sigmoid(cols['g_a'].astype(F32))
    g_b = jax.nn.sigmoid(cols['g_b'].astype(F32))
    x1 = x + (g_a * h_a + g_b * h_b).astype(x.dtype) @ p['w_out']
    f = peer(rmsnorm(x1, p['norm_ffn']).reshape(B * T, D), p['peer_wq'], p['peer_subkeys'], p['peer_u'], p['peer_v'])
    return x1 + f.reshape(B, T, D)


def setup_inputs(seed: int = 0) -> dict:
    key = jax.random.key(seed)
    keys = list(jax.random.split(key, 40))
    counter = [0]

    def nxt():
        counter[0] += 1
        return keys[counter[0] - 1]

    def nrm(shape, scale=1.0):
        return jax.random.normal(nxt(), shape, F32) * scale

    def gain(shape):
        return 1.0 + nrm(shape, 0.02)

    n_pages = PAST_LEN // PAGE_SIZE
    n_pool = (5 * DEC_BATCH * n_pages) // 4
    n_win = min(WINDOW, PAST_LEN)
    in_width = sum(w for _, w in IN_LAYOUT)
    qk_width = 2 * M_HEADS * M_DQK
    pool_shape = (DEPTH, n_pool, PAGE_SIZE, N_KV, N_HD)
    x_prompt = nrm((BATCH, SEQ, D_MODEL))
    x_sample = nrm((DEC_BATCH, DEC_SEQ, D_MODEL))
    cache_cmp_k = nrm(pool_shape)
    cache_cmp_v = nrm(pool_shape)
    cache_slc_k = nrm(pool_shape)
    cache_slc_v = nrm(pool_shape)
    state_win_k = nrm((DEPTH, DEC_BATCH, n_win, N_KV, N_HD))
    state_win_v = nrm((DEPTH, DEC_BATCH, n_win, N_KV, N_HD))
    state_conv = nrm((DEPTH, DEC_BATCH, M_CONV - 1, qk_width))
    state_C = nrm((DEPTH, DEC_BATCH, M_HEADS, M_DQK, M_DV))
    state_n = nrm((DEPTH, DEC_BATCH, M_HEADS, M_DQK))
    state_m = nrm((DEPTH, DEC_BATCH, M_HEADS))
    perm = jax.random.permutation(nxt(), n_pool)
    page_table = perm[:DEC_BATCH * n_pages].reshape(DEC_BATCH, n_pages).astype(jnp.int32)
    return {
        'x_prompt': x_prompt,
        'x_sample': x_sample,
        'cache_cmp_k': cache_cmp_k,
        'cache_cmp_v': cache_cmp_v,
        'cache_slc_k': cache_slc_k,
        'cache_slc_v': cache_slc_v,
        'state_win_k': state_win_k,
        'state_win_v': state_win_v,
        'state_conv': state_conv,
        'state_C': state_C,
        'state_n': state_n,
        'state_m': state_m,
        'page_table': page_table,
        'norm_mix': gain((DEPTH, D_MODEL)),
        'w_in': nrm((DEPTH, D_MODEL, in_width), D_MODEL ** -0.5),
        'm_conv_w': nrm((DEPTH, M_CONV, qk_width), M_CONV ** -0.5),
        'm_conv_b': nrm((DEPTH, qk_width), 0.02),
        'm_bias_i': nrm((DEPTH, M_HEADS), 0.1),
        'm_bias_f': jnp.linspace(3.0, 6.0, M_HEADS, dtype=F32)[None, :] + nrm((DEPTH, M_HEADS), 0.1),
        'm_norm': gain((DEPTH, M_HEADS * M_DV)),
        'cmp_k_w1': nrm((DEPTH, CMP_BLOCK, N_HD, CMP_HIDDEN), (CMP_BLOCK * N_HD) ** -0.5),
        'cmp_k_b1': nrm((DEPTH, CMP_HIDDEN), 0.02),
        'cmp_k_w2': nrm((DEPTH, CMP_HIDDEN, N_HD), CMP_HIDDEN ** -0.5),
        'cmp_v_w1': nrm((DEPTH, CMP_BLOCK, N_HD, CMP_HIDDEN), (CMP_BLOCK * N_HD) ** -0.5),
        'cmp_v_b1': nrm((DEPTH, CMP_HIDDEN), 0.02),
        'cmp_v_w2': nrm((DEPTH, CMP_HIDDEN, N_HD), CMP_HIDDEN ** -0.5),
        'w_out': nrm((DEPTH, D_MODEL, D_MODEL), D_MODEL ** -0.5),
        'norm_ffn': gain((DEPTH, D_MODEL)),
        'peer_wq': nrm((DEPTH, D_MODEL, P_HEADS * P_DKEY), D_MODEL ** -0.5),
        'peer_subkeys': nrm((DEPTH, P_HEADS, 2, P_NKEYS, P_DKEY // 2), (P_DKEY // 2) ** -0.5),
        'peer_u': nrm((DEPTH, P_EXPERTS, D_MODEL), D_MODEL ** -0.5),
        'peer_v': nrm((DEPTH, P_EXPERTS, D_MODEL), 0.5),
        'norm_final': gain((D_MODEL,)),
    }


def reference(x_prompt, x_sample, cache_cmp_k, cache_cmp_v, cache_slc_k, cache_slc_v, state_win_k, state_win_v,
              state_conv, state_C, state_n, state_m, page_table, norm_mix, w_in, m_conv_w, m_conv_b, m_bias_i,
              m_bias_f, m_norm, cmp_k_w1, cmp_k_b1, cmp_k_w2, cmp_v_w1, cmp_v_b1, cmp_v_w2, w_out, norm_ffn,
              peer_wq, peer_subkeys, peer_u, peer_v, norm_final):
    T = x_prompt.shape[1]
    past = page_table.shape[1] * PAGE_SIZE
    pos_p = jnp.arange(T, dtype=jnp.int32)
    pos_s = past + jnp.arange(x_sample.shape[1], dtype=jnp.int32)
    n_win_p = min(WINDOW, T)
    pools = (cache_cmp_k, cache_cmp_v, cache_slc_k, cache_slc_v)
    xp, xs = x_prompt, x_sample
    per_layer = []
    for l in range(DEPTH):
        p = {
            'norm_mix': norm_mix[l], 'w_in': w_in[l], 'm_conv_w': m_conv_w[l], 'm_conv_b': m_conv_b[l],
            'm_bias_i': m_bias_i[l], 'm_bias_f': m_bias_f[l], 'm_norm': m_norm[l],
            'cmp_k_w1': cmp_k_w1[l], 'cmp_k_b1': cmp_k_b1[l], 'cmp_k_w2': cmp_k_w2[l],
            'cmp_v_w1': cmp_v_w1[l], 'cmp_v_b1': cmp_v_b1[l], 'cmp_v_w2': cmp_v_w2[l],
            'w_out': w_out[l], 'norm_ffn': norm_ffn[l], 'peer_wq': peer_wq[l], 'peer_subkeys': peer_subkeys[l],
            'peer_u': peer_u[l], 'peer_v': peer_v[l],
        }
        bp = xp.shape[0]
        cols, q, kv_p, gates = mixer_inputs(xp, pos_p, p)
        h_a, conv_p, C_p, n_p, m_p = mlstm_branch(
            cols, jnp.zeros((bp, M_CONV - 1, 2 * M_HEADS * M_DQK), state_conv.dtype),
            jnp.zeros((bp, M_HEADS, M_DQK, M_DV), state_C.dtype), jnp.zeros((bp, M_HEADS, M_DQK), state_n.dtype),
            jnp.zeros((bp, M_HEADS), state_m.dtype), p)
        h_b = nsa_prompt(q, kv_p, gates, p)
        xp = merge_and_channel_mix(xp, cols, h_a, h_b, p)
        cols, q, kv_s, gates = mixer_inputs(xs, pos_s, p)
        h_a, conv_s, C_s, n_s, m_s = mlstm_branch(cols, state_conv[l], state_C[l], state_n[l], state_m[l], p)
        h_b, wk_s, wv_s = nsa_sample(q, kv_s, gates, pools, l, page_table, state_win_k[l], state_win_v[l], p)
        xs = merge_and_channel_mix(xs, cols, h_a, h_b, p)
        per_layer.append((kv_p[0], kv_s[0], kv_p[1], kv_s[1], kv_p[2], kv_s[2], kv_p[3], kv_s[3],
                          kv_p[4][:, -n_win_p:], wk_s, kv_p[5][:, -n_win_p:], wv_s,
                          conv_p, conv_s, C_p, C_s, n_p, n_s, m_p, m_s))
    (cmp_k_p, cmp_k_s, cmp_v_p, cmp_v_s, slc_k_p, slc_k_s, slc_v_p, slc_v_s,
     win_k_p, win_k_s, win_v_p, win_v_s, conv_p, conv_s, C_p, C_s, n_p, n_s, m_p, m_s) = [
        jnp.stack(entry) for entry in zip(*per_layer)]
    y_prompt = rmsnorm(xp, norm_final)
    y_sample = rmsnorm(xs, norm_final)
    return (y_prompt, y_sample, cmp_k_p, cmp_k_s, cmp_v_p, cmp_v_s, slc_k_p, slc_k_s, slc_v_p, slc_v_s,
            win_k_p, win_k_s, win_v_p, win_v_s, conv_p, conv_s, C_p, C_s, n_p, n_s, m_p, m_s)
```

```python
import functools

import jax
import jax.numpy as jnp
from jax import lax
from jax.experimental import pallas as pl
from jax.experimental.pallas import tpu as pltpu

F32 = jnp.float32
MXU_DTYPE = jnp.bfloat16
HIGHEST = lax.Precision.HIGHEST
EPS = 1e-6
NEG = -1e30

D_MODEL = 1024
PAGE_SIZE = 128
M_HEADS, M_DQK, M_DV, M_CONV, M_CHUNK = 4, 128, 256, 4, 64
N_HEADS, N_KV, N_GROUP, N_HD = 16, 4, 4, 64
CMP_BLOCK, CMP_HIDDEN, SLC_BLOCK, SLC_TOPK, WINDOW, Q_BLOCK = 32, 128, 64, 16, 512, 128
ROPE_THETA, ROPE_DIM = 500000.0, 16
P_HEADS, P_NKEYS, P_DKEY, P_TOPK = 8, 128, 256, 16

OFF_MQ, OFF_MK, OFF_MV, OFF_MO, OFF_NQ = 0, 512, 1024, 2048, 3072
OFF_KC, OFF_VC, OFF_KS, OFF_VS, OFF_KW, OFF_VW = 4096, 4352, 4608, 4864, 5120, 5376
OFF_GA, OFF_GB, Z_MAIN = 5632, 6656, 7680
SMALL_W = 128
SM_I, SM_F, SM_G = 0, 4, 8
ROW_TILE = 512
COL_TILE = 512
SAMPLE_PAD = 8
VMEM_LIMIT = 56 * 1024 * 1024


def _cparams(sem):
    return pltpu.CompilerParams(dimension_semantics=sem, vmem_limit_bytes=VMEM_LIMIT)


def _mm(a, b):
    return jnp.dot(a.astype(MXU_DTYPE), b.astype(MXU_DTYPE), preferred_element_type=F32)


def _mm_nt(a, b):
    return lax.dot_general(a.astype(MXU_DTYPE), b.astype(MXU_DTYPE), (((1,), (1,)), ((), ())),
                           preferred_element_type=F32)


def _mm_tn(a, b):
    return lax.dot_general(a.astype(MXU_DTYPE), b.astype(MXU_DTYPE), (((0,), (0,)), ((), ())),
                           preferred_element_type=F32)


def _log_sigmoid(x):
    return jnp.minimum(x, 0.0) - jnp.log(1.0 + jnp.exp(-jnp.abs(x)))


def _sigmoid(x):
    return 1.0 / (1.0 + jnp.exp(-x))


def _col_to_row(x_col):
    L = x_col.shape[0]
    ti = lax.broadcasted_iota(jnp.int32, (L, L), 0)
    si = lax.broadcasted_iota(jnp.int32, (L, L), 1)
    return jnp.sum(jnp.where(ti == si, x_col, 0.0), axis=0, keepdims=True)


def _inproj_kernel(x_ref, g_ref, w_ref, ws_ref, cos_ref, sin_ref, z_ref, zs_ref, xn_scr):
    j = pl.program_id(1)

    @pl.when(j == 0)
    def _():
        x = x_ref[...]
        r = x * lax.rsqrt(jnp.mean(x * x, axis=-1, keepdims=True) + EPS) * g_ref[...]
        xn = r.astype(MXU_DTYPE)
        xn_scr[...] = xn
        zs_ref[...] = jnp.dot(xn, ws_ref[...], preferred_element_type=F32)

    acc = jnp.dot(xn_scr[...], w_ref[...], preferred_element_type=F32)
    rope_full = (j == OFF_NQ // COL_TILE) | (j == OFF_NQ // COL_TILE + 1)
    rope_half = (j == OFF_KS // COL_TILE) | (j == OFF_KW // COL_TILE)
    is_rope = rope_full | rope_half

    @pl.when(is_rope)
    def _():
        lane = lax.broadcasted_iota(jnp.int32, acc.shape, 1)
        on = lane < jnp.where(rope_full, COL_TILE, COL_TILE // 2)
        reps = COL_TILE // cos_ref.shape[1]
        cos = jnp.where(on, jnp.concatenate([cos_ref[...]] * reps, axis=1), 1.0)
        sin = jnp.where(on, jnp.concatenate([sin_ref[...]] * reps, axis=1), 0.0)
        half = ROPE_DIM // 2
        partner = jnp.where(lane % N_HD < half, pltpu.roll(acc, COL_TILE - half, 1), pltpu.roll(acc, half, 1))
        z_ref[...] = acc * cos + partner * sin

    @pl.when(jnp.logical_not(is_rope))
    def _():
        z_ref[...] = acc


def _inproj(x_all, gain, w_main, w_small, cos_t, sin_t):
    rows = x_all.shape[0]
    grid = (rows // ROW_TILE, Z_MAIN // COL_TILE)
    return pl.pallas_call(
        _inproj_kernel,
        grid=grid,
        in_specs=[
            pl.BlockSpec((ROW_TILE, D_MODEL), lambda i, j: (i, 0)),
            pl.BlockSpec((1, D_MODEL), lambda i, j: (0, 0)),
            pl.BlockSpec((D_MODEL, COL_TILE), lambda i, j: (0, j)),
            pl.BlockSpec((D_MODEL, SMALL_W), lambda i, j: (0, 0)),
            pl.BlockSpec((ROW_TILE, 128), lambda i, j: (i, 0)),
            pl.BlockSpec((ROW_TILE, 128), lambda i, j: (i, 0)),
        ],
        out_specs=[
            pl.BlockSpec((ROW_TILE, COL_TILE), lambda i, j: (i, j)),
            pl.BlockSpec((ROW_TILE, SMALL_W), lambda i, j: (i, 0)),
        ],
        out_shape=[
            jax.ShapeDtypeStruct((rows, Z_MAIN), F32),
            jax.ShapeDtypeStruct((rows, SMALL_W), F32),
        ],
        scratch_shapes=[pltpu.VMEM((ROW_TILE, D_MODEL), MXU_DTYPE)],
        compiler_params=_cparams(("parallel", "arbitrary")),
        name="inproj",
    )(x_all, gain, w_main, w_small, cos_t, sin_t)


def _rope_tables(pos):
    half = ROPE_DIM // 2
    inv = ROPE_THETA ** (-jnp.arange(half, dtype=F32) / half)
    ang = pos.astype(F32)[:, None] * inv
    cos, sin = jnp.cos(ang), jnp.sin(ang)
    rows = pos.shape[0]
    pad1 = jnp.ones((rows, N_HD - ROPE_DIM), F32)
    pad0 = jnp.zeros((rows, N_HD - ROPE_DIM), F32)
    cos_h = jnp.concatenate([cos, cos, pad1], axis=1)
    sin_h = jnp.concatenate([-sin, sin, pad0], axis=1)
    return jnp.concatenate([cos_h, cos_h], axis=1), jnp.concatenate([sin_h, sin_h], axis=1)


def _mlstm_kernel(zq_ref, zk_ref, zv_ref, zo_ref, zs_ref, hq_ref, hk_ref, cwq_ref, cwk_ref, cbq_ref, cbk_ref,
                  brow_ref, norm_ref, c0_ref, n0_ref, m0_ref,
                  h_ref, c_ref, n_ref, m_ref, c_scr, n_scr, m_scr, hq_scr, hk_scr, *, chunk, nvalid):
    t = pl.program_id(2)
    h = pl.program_id(1)
    rb = zq_ref.shape[0]

    @pl.when(t == 0)
    def _():
        c_scr[...] = c0_ref[0, 0]
        n_scr[...] = n0_ref[0, 0]
        m_scr[...] = m0_ref[0, 0]
        hq_scr[...] = hq_ref[0]
        hk_scr[...] = hk_ref[0]

    def conv_silu(z_ref_, hist_scr, cw_ref, cb_ref):
        z = z_ref_[...]
        ext = jnp.concatenate([hist_scr[...], z], axis=0)
        acc = cb_ref[...]
        for jj in range(M_CONV):
            off = 8 - (M_CONV - 1) + jj
            acc = acc + ext[off:off + rb] * cw_ref[jj:jj + 1, :]
        hist_scr[...] = z[rb - 8:rb]
        return acc * _sigmoid(acc)

    q_all = conv_silu(zq_ref, hq_scr, cwq_ref, cbq_ref)
    k_all = conv_silu(zk_ref, hk_scr, cwk_ref, cbk_ref) * (M_DQK ** -0.5)

    zs = zs_ref[...] + brow_ref[...]
    col = lax.broadcasted_iota(jnp.int32, (rb, SMALL_W), 1)
    li_cols = jnp.sum(jnp.where(col == SM_I + h, zs, 0.0), axis=1, keepdims=True)
    lf_cols = _log_sigmoid(jnp.sum(jnp.where(col == SM_F + h, zs, 0.0), axis=1, keepdims=True))

    L = chunk
    ti = lax.broadcasted_iota(jnp.int32, (L, L), 0)
    si = lax.broadcasted_iota(jnp.int32, (L, L), 1)
    mask = (si <= ti) & (si < nvalid)
    tril = (si <= ti).astype(F32)
    tcol = lax.broadcasted_iota(jnp.int32, (L, 1), 0)

    C = c_scr[...]
    n = n_scr[...]
    m = m_scr[:, 0:1]
    gain = norm_ref[...]
    for c in range(rb // L):
        sl = slice(c * L, (c + 1) * L)
        q, k, v = q_all[sl], k_all[sl], zv_ref[sl, :]
        li_c = li_cols[sl]
        lf_c = jnp.where(tcol < nvalid, lf_cols[sl], 0.0)
        b_c = jnp.dot(tril, jnp.broadcast_to(lf_c, (L, 128)), precision=HIGHEST, preferred_element_type=F32)[:, 0:1]
        b_r = _col_to_row(b_c)
        li_r = _col_to_row(li_c)
        logw = jnp.where(mask, b_c - b_r + li_r, NEG)
        inter = b_c + m
        m_t = jnp.maximum(inter, jnp.max(logw, axis=1, keepdims=True))
        w = jnp.where(mask, jnp.exp(logw - m_t), 0.0) * _mm_nt(q, k)
        w_inter = jnp.exp(inter - m_t)
        num = _mm(w, v) + w_inter * _mm(q, C)
        den = jnp.sum(w, axis=1, keepdims=True) + w_inter * jnp.sum(q * n, axis=1, keepdims=True)
        hh = num / jnp.maximum(jnp.abs(den), jnp.exp(-m_t))
        b_end = b_c[L - 1:L]
        m_new = m_t[L - 1:L]
        w_end = jnp.where(tcol < nvalid, jnp.exp(b_end - b_c + li_c - m_new), 0.0)
        decay = jnp.exp(b_end + m - m_new)
        kw = w_end * k
        C = decay * C + _mm_tn(kw, v)
        n = decay * n + jnp.sum(kw, axis=0, keepdims=True)
        m = m_new
        hh = hh * lax.rsqrt(jnp.mean(hh * hh, axis=-1, keepdims=True) + EPS) * gain
        h_ref[sl, :] = _sigmoid(zo_ref[sl, :]) * hh

    c_scr[...] = C
    n_scr[...] = n
    m_scr[...] = jnp.broadcast_to(m, m_scr.shape)

    @pl.when(t == pl.num_programs(2) - 1)
    def _():
        c_ref[0, 0] = C
        n_ref[0, 0] = n
        m_ref[0, 0] = jnp.broadcast_to(m, m_scr.shape)


def _mlstm(z, zs, row0, nb, rows_per_seq, rb, chunk, nvalid, hist, conv_w, conv_b, brow, norm, c0, n0, m0):
    assert row0 % rb == 0 and rows_per_seq % rb == 0 and rb % chunk == 0
    nt = rows_per_seq // rb
    rblk = lambda b, t: row0 // rb + b * nt + t
    kern = functools.partial(_mlstm_kernel, chunk=chunk, nvalid=nvalid)
    return pl.pallas_call(
        kern,
        grid=(nb, M_HEADS, nt),
        in_specs=[
            pl.BlockSpec((rb, M_DQK), lambda b, h, t: (rblk(b, t), OFF_MQ // M_DQK + h)),
            pl.BlockSpec((rb, M_DQK), lambda b, h, t: (rblk(b, t), OFF_MK // M_DQK + h)),
            pl.BlockSpec((rb, M_DV), lambda b, h, t: (rblk(b, t), OFF_MV // M_DV + h)),
            pl.BlockSpec((rb, M_DV), lambda b, h, t: (rblk(b, t), OFF_MO // M_DV + h)),
            pl.BlockSpec((rb, SMALL_W), lambda b, h, t: (rblk(b, t), 0)),
            pl.BlockSpec((1, 8, M_DQK), lambda b, h, t: (b, 0, OFF_MQ // M_DQK + h)),
            pl.BlockSpec((1, 8, M_DQK), lambda b, h, t: (b, 0, OFF_MK // M_DQK + h)),
            pl.BlockSpec((M_CONV, M_DQK), lambda b, h, t: (0, h)),
            pl.BlockSpec((M_CONV, M_DQK), lambda b, h, t: (0, M_HEADS + h)),
            pl.BlockSpec((1, M_DQK), lambda b, h, t: (0, h)),
            pl.BlockSpec((1, M_DQK), lambda b, h, t: (0, M_HEADS + h)),
            pl.BlockSpec((1, SMALL_W), lambda b, h, t: (0, 0)),
            pl.BlockSpec((1, M_DV), lambda b, h, t: (0, h)),
            pl.BlockSpec((1, 1, M_DQK, M_DV), lambda b, h, t: (b, h, 0, 0)),
            pl.BlockSpec((1, 1, 1, M_DQK), lambda b, h, t: (b, h, 0, 0)),
            pl.BlockSpec((1, 1, 1, 128), lambda b, h, t: (b, h, 0, 0)),
        ],
        out_specs=[
            pl.BlockSpec((rb, M_DV), lambda b, h, t: (b * nt + t, h)),
            pl.BlockSpec((1, 1, M_DQK, M_DV), lambda b, h, t: (b, h, 0, 0)),
            pl.BlockSpec((1, 1, 1, M_DQK), lambda b, h, t: (b, h, 0, 0)),
            pl.BlockSpec((1, 1, 1, 128), lambda b, h, t: (b, h, 0, 0)),
        ],
        out_shape=[
            jax.ShapeDtypeStruct((nb * rows_per_seq, M_HEADS * M_DV), F32),
            jax.ShapeDtypeStruct((nb, M_HEADS, M_DQK, M_DV), F32),
            jax.ShapeDtypeStruct((nb, M_HEADS, 1, M_DQK), F32),
            jax.ShapeDtypeStruct((nb, M_HEADS, 1, 128), F32),
        ],
        scratch_shapes=[
            pltpu.VMEM((M_DQK, M_DV), F32),
            pltpu.VMEM((1, M_DQK), F32),
            pltpu.VMEM((1, 128), F32),
            pltpu.VMEM((8, M_DQK), F32),
            pltpu.VMEM((8, M_DQK), F32),
        ],
        compiler_params=_cparams(("parallel", "parallel", "arbitrary")),
        name="mlstm",
    )(z, z, z, z, zs, hist, hist, conv_w, conv_w, conv_b, conv_b, brow, norm, c0, n0, m0)


KV_W = N_KV * N_HD
HID_W = N_KV * CMP_HIDDEN


def _gelu(x):
    return 0.5 * x * (1.0 + lax.erf(x * (2.0 ** -0.5)))


def _rope_rows(x, cos, sin):
    lane = lax.broadcasted_iota(jnp.int32, x.shape, 1)
    half = ROPE_DIM // 2
    w = x.shape[1]
    partner = jnp.where(lane % N_HD < half, pltpu.roll(x, w - half, 1), pltpu.roll(x, half, 1))
    reps = w // cos.shape[1]
    return x * jnp.concatenate([cos] * reps, axis=1) + partner * jnp.concatenate([sin] * reps, axis=1)


def _store_heads(o_ref, row0, val):
    for kv in range(N_KV):
        o_ref[0, kv, row0:row0 + val.shape[0], :] = val[:, kv * N_HD:(kv + 1) * N_HD]


def _compress_blocks(lo_ref, hi_ref, nb, w1_ref, b1_ref, w2_ref):
    hid = jnp.zeros((nb, HID_W), F32)
    for j in range(CMP_BLOCK):
        xj = jnp.concatenate([lo_ref[pl.ds(j, nb, stride=CMP_BLOCK), :], hi_ref[pl.ds(j, nb, stride=CMP_BLOCK), :]], axis=1)
        hid = hid + _mm(xj, w1_ref[j])
    return _mm(_gelu(hid + b1_ref[...]), w2_ref[...])


def _compress_prompt_kernel(lo_ref, hi_ref, w1_ref, b1_ref, w2_ref, cos_ref, sin_ref, o_ref, *, nb):
    out = _compress_blocks(lo_ref, hi_ref, nb, w1_ref, b1_ref, w2_ref)
    _store_heads(o_ref, 0, _rope_rows(out, cos_ref[0:nb, :], sin_ref[0:nb, :]))


def _compress_prompt(z, B, T, col_off, wts, tabs):
    nb = T // CMP_BLOCK
    cos, sin = tabs
    w1, b1, w2 = wts
    kern = functools.partial(_compress_prompt_kernel, nb=nb)
    c2 = lambda b: (0, 0)
    return pl.pallas_call(
        kern,
        grid=(B,),
        in_specs=[pl.BlockSpec((T, 128), lambda b: (b, col_off // 128)),
                  pl.BlockSpec((T, 128), lambda b: (b, col_off // 128 + 1)),
                  pl.BlockSpec((CMP_BLOCK, KV_W, HID_W), lambda b: (0, 0, 0)),
                  pl.BlockSpec((1, HID_W), c2), pl.BlockSpec((HID_W, KV_W), c2),
                  pl.BlockSpec(cos.shape, c2), pl.BlockSpec(sin.shape, c2)],
        out_specs=pl.BlockSpec((1, N_KV, nb, N_HD), lambda b: (b, 0, 0, 0)),
        out_shape=jax.ShapeDtypeStruct((B, N_KV, nb, N_HD), F32),
        compiler_params=_cparams(("parallel",)),
        name="compress_prompt",
    )(z, z, w1, b1, w2, cos, sin)


def _page_copy(pt_ref, b, p, pool_ref, buf_ref, sem, rows_per_page):
    return pltpu.make_async_copy(pool_ref.at[pt_ref[b, p]],
                                 buf_ref.at[pl.ds(pl.multiple_of(p * rows_per_page, rows_per_page), rows_per_page)], sem)


def _page_half_copy(pt_ref, b, p, pool_ref, half, buf_ref, sem):
    return pltpu.make_async_copy(pool_ref.at[pt_ref[b, p], :, pl.ds(half * 128, 128)],
                                 buf_ref.at[pl.ds(pl.multiple_of(p * PAGE_SIZE, PAGE_SIZE), PAGE_SIZE)], sem)


def _compress_sample_kernel(pt_ref, pool_ref, new_ref, w1_ref, b1_ref, w2_ref, cos_ref, sin_ref, o_ref,
                            lo_buf, hi_buf, sem, *, n_pages, nb, n_new):
    b = pl.program_id(0)

    def start(p, c):
        _page_half_copy(pt_ref, b, p, pool_ref, 0, lo_buf, sem).start()
        _page_half_copy(pt_ref, b, p, pool_ref, 1, hi_buf, sem).start()
        return c

    def wait(p, c):
        _page_half_copy(pt_ref, b, p, pool_ref, 0, lo_buf, sem).wait()
        _page_half_copy(pt_ref, b, p, pool_ref, 1, hi_buf, sem).wait()
        return c

    lax.fori_loop(0, n_pages, start, 0)
    o_ref[...] = jnp.zeros(o_ref.shape, o_ref.dtype)
    xn = new_ref[...]
    hid = jnp.zeros((8, HID_W), F32)
    for jrow in range(n_new):
        hid = hid + _mm(jnp.broadcast_to(xn[jrow:jrow + 1, :], (8, KV_W)), w1_ref[jrow])
    row = lax.broadcasted_iota(jnp.int32, (8, HID_W), 0)
    hid = jnp.where(row == 0, hid, 0.0) + b1_ref[...]
    new = _mm(_gelu(hid), w2_ref[...])
    _store_heads(o_ref, nb, _rope_rows(new, cos_ref[nb:nb + 8, :], sin_ref[nb:nb + 8, :]))
    lax.fori_loop(0, n_pages, wait, 0)
    out = _compress_blocks(lo_buf, hi_buf, nb, w1_ref, b1_ref, w2_ref)
    _store_heads(o_ref, 0, _rope_rows(out, cos_ref[0:nb, :], sin_ref[0:nb, :]))


def _compress_sample(page_table, pool, z, new_row_blk0, new_col_blk, wts, tabs, nbp, n_new):
    BS, n_pages = page_table.shape
    nb = n_pages * PAGE_SIZE // CMP_BLOCK
    cos, sin = tabs
    w1, b1, w2 = wts
    kern = functools.partial(_compress_sample_kernel, n_pages=n_pages, nb=nb, n_new=n_new)
    c2 = lambda b, pt: (0, 0)
    gs = pltpu.PrefetchScalarGridSpec(
        num_scalar_prefetch=1,
        grid=(BS,),
        in_specs=[pl.BlockSpec(memory_space=pl.ANY),
                  pl.BlockSpec((8, KV_W), lambda b, pt: (new_row_blk0 + b, new_col_blk)),
                  pl.BlockSpec((CMP_BLOCK, KV_W, HID_W), lambda b, pt: (0, 0, 0)),
                  pl.BlockSpec((1, HID_W), c2), pl.BlockSpec((HID_W, KV_W), c2),
                  pl.BlockSpec(cos.shape, c2), pl.BlockSpec(sin.shape, c2)],
        out_specs=pl.BlockSpec((1, N_KV, nbp, N_HD), lambda b, pt: (b, 0, 0, 0)),
        scratch_shapes=[pltpu.VMEM((n_pages * PAGE_SIZE, 128), F32), pltpu.VMEM((n_pages * PAGE_SIZE, 128), F32),
                        pltpu.SemaphoreType.DMA(())],
    )
    return pl.pallas_call(
        kern,
        grid_spec=gs,
        out_shape=jax.ShapeDtypeStruct((BS, N_KV, nbp, N_HD), F32),
        compiler_params=_cparams(("arbitrary",)),
        name="compress_sample",
    )(page_table, pool, z, w1, b1, w2, cos, sin)


def _compress_weights(w1, b1, w2):
    eye = jnp.eye(N_KV, dtype=w1.dtype)
    w1big = jnp.einsum('jdh,kl->jkdlh', w1, eye).reshape(CMP_BLOCK, KV_W, HID_W)
    w2big = jnp.einsum('hd,kl->khld', w2, eye).reshape(HID_W, KV_W)
    return w1big.astype(MXU_DTYPE), jnp.tile(b1, N_KV).reshape(1, HID_W).astype(F32), w2big.astype(MXU_DTYPE)


def _block_tables(nbp, rope):
    cos, sin = _rope_tables(jnp.arange(nbp, dtype=jnp.int32) * CMP_BLOCK + CMP_BLOCK - 1)
    if not rope:
        cos, sin = jnp.ones_like(cos), jnp.zeros_like(sin)
    return cos, sin


SCALE = N_HD ** -0.5
INF = float('inf')


def _masked_softmax(s, mask):
    s = jnp.where(mask, s, NEG)
    m = jnp.max(s, axis=1, keepdims=True)
    e = jnp.where(mask, jnp.exp(s - m), 0.0)
    return e / jnp.maximum(jnp.sum(e, axis=1, keepdims=True), 1.0)


def _topk_mask(score, k):
    rows, width = score.shape
    lane = lax.broadcasted_iota(jnp.int32, (rows, width), 1).astype(F32)
    sel = jnp.zeros((rows, width), F32)
    s = score
    for _ in range(k):
        m = jnp.max(s, axis=1, keepdims=True)
        idx = jnp.min(jnp.where(s == m, lane, float(width)), axis=1, keepdims=True)
        hit = lane == idx
        sel = jnp.where(hit, 1.0, sel)
        s = jnp.where(hit, -INF, s)
    return sel


def _stack_heads(q, tile):
    return jnp.concatenate([q[:, g * N_HD:(g + 1) * N_HD] for g in range(N_GROUP)], axis=0)


def _compressed_branch(qs, kc, vc, tq, nbc, wp, tile):
    nbp = kc.shape[0]
    blk = lax.broadcasted_iota(jnp.int32, (1, nbp), 1)
    s_c = _mm_nt(qs, kc) * SCALE
    mask_c = (blk * CMP_BLOCK + CMP_BLOCK - 1 <= tq) & (blk < nbc)
    p_c = _masked_softmax(s_c, mask_c)
    o_cmp = _mm(p_c, vc)
    psum = p_c[0:tile]
    for g in range(1, N_GROUP):
        psum = psum + p_c[g * tile:(g + 1) * tile]
    ratio = SLC_BLOCK // CMP_BLOCK
    pair = jnp.where(lax.broadcasted_iota(jnp.int32, (nbp, wp), 0) // ratio
                     == lax.broadcasted_iota(jnp.int32, (nbp, wp), 1), 1.0, 0.0)
    return o_cmp, jnp.dot(psum, pair, precision=HIGHEST, preferred_element_type=F32)


def _select_blocks(imp, tq_t, n_sel, wp):
    sb = lax.broadcasted_iota(jnp.int32, (1, wp), 1)
    valid = (sb * SLC_BLOCK <= tq_t) & (sb < n_sel)
    forced = (sb == 0) | (sb == tq_t // SLC_BLOCK)
    score = jnp.where(valid, jnp.where(forced, INF, imp), -INF)
    return _topk_mask(score, min(SLC_TOPK, n_sel))


def _expand_sel(sel, kpos, wp):
    sbc = lax.broadcasted_iota(jnp.int32, (wp, 1), 0)
    e = jnp.where(sbc == kpos // SLC_BLOCK, 1.0, 0.0)
    return jnp.dot(sel.astype(jnp.bfloat16), e.astype(jnp.bfloat16), preferred_element_type=F32)


def _online_step(carry, s, mask, v):
    m, l, acc = carry
    s = jnp.where(mask, s, NEG)
    m_new = jnp.maximum(m, jnp.max(s, axis=1, keepdims=True))
    alpha = jnp.exp(m - m_new)
    p = jnp.where(mask, jnp.exp(s - m_new), 0.0)
    return m_new, alpha * l + jnp.sum(p, axis=1, keepdims=True), alpha * acc + _mm(p, v)


def _gate_col(gates, idx):
    col = lax.broadcasted_iota(jnp.int32, gates.shape, 1)
    return jnp.sum(jnp.where(col == idx, gates, 0.0), axis=1, keepdims=True)


def _combine(gates, head0, o_cmp, o_slc, o_win, tile):
    outs = []
    for g in range(N_GROUP):
        rows = slice(g * tile, (g + 1) * tile)
        c0 = SM_G + (head0 + g) * 3
        outs.append(_gate_col(gates, c0) * o_cmp[rows] + _gate_col(gates, c0 + 1) * o_slc[rows]
                    + _gate_col(gates, c0 + 2) * o_win[rows])
    return jnp.concatenate(outs, axis=1)


SLC_TILE = 256
WIN_SPAN = WINDOW + Q_BLOCK


def _nsa_prompt_kernel(q_ref, zs_ref, kc_ref, vc_ref, ks_ref, vs_ref, kw_ref, vw_ref, o_ref, *, nbc, n_sel, wp):
    kv = pl.program_id(1)
    i = pl.program_id(2)
    tile = Q_BLOCK
    qs = _stack_heads(q_ref[...], tile)
    trow = lax.broadcasted_iota(jnp.int32, (N_GROUP * tile, 1), 0) % tile
    tq = i * tile + trow
    tq_t = i * tile + lax.broadcasted_iota(jnp.int32, (tile, 1), 0)

    o_cmp, imp = _compressed_branch(qs, kc_ref[0, 0], vc_ref[0, 0], tq, nbc, wp, tile)
    sel = _select_blocks(imp, tq_t, n_sel, wp)

    def slc_step(j, carry):
        k0 = pl.multiple_of(j * SLC_TILE, SLC_TILE)
        kpos = k0 + lax.broadcasted_iota(jnp.int32, (1, SLC_TILE), 1)
        s = _mm_nt(qs, ks_ref[0, 0, pl.ds(k0, SLC_TILE), :]) * SCALE
        mt = (_expand_sel(sel, kpos, wp) > 0.5) & (kpos <= tq_t)
        mask = jnp.concatenate([mt] * N_GROUP, axis=0)
        return _online_step(carry, s, mask, vs_ref[0, 0, pl.ds(k0, SLC_TILE), :])

    rows = N_GROUP * tile
    init = (jnp.full((rows, 1), NEG, F32), jnp.zeros((rows, 1), F32), jnp.zeros((rows, N_HD), F32))
    n_tiles = (i * tile + tile - 1) // SLC_TILE + 1
    _, l, acc = lax.fori_loop(0, n_tiles, slc_step, init)
    o_slc = acc / jnp.maximum(l, 1.0)

    w0 = pl.multiple_of(jnp.maximum(i * tile - WINDOW, 0), tile)
    pw = w0 + lax.broadcasted_iota(jnp.int32, (1, WIN_SPAN), 1)
    dist = tq - pw
    s_w = _mm_nt(qs, kw_ref[0, 0, pl.ds(w0, WIN_SPAN), :]) * SCALE
    p_w = _masked_softmax(s_w, (dist >= 0) & (dist < WINDOW))
    o_win = _mm(p_w, vw_ref[0, 0, pl.ds(w0, WIN_SPAN), :])

    o_ref[...] = _combine(_sigmoid(zs_ref[...]), kv * N_GROUP, o_cmp, o_slc, o_win, tile)


def _nsa_prompt(z, zs, kc, vc, ks, vs, kw, vw, B, T, nbc, n_sel, wp):
    nq = T // Q_BLOCK
    nbp = kc.shape[2]
    kern = functools.partial(_nsa_prompt_kernel, nbc=nbc, n_sel=n_sel, wp=wp)
    seq = lambda b, kv, i: (b, kv, 0, 0)
    return pl.pallas_call(
        kern,
        grid=(B, N_KV, nq),
        in_specs=[pl.BlockSpec((Q_BLOCK, KV_W), lambda b, kv, i: (b * nq + i, OFF_NQ // KV_W + kv)),
                  pl.BlockSpec((Q_BLOCK, SMALL_W), lambda b, kv, i: (b * nq + i, 0)),
                  pl.BlockSpec((1, 1, nbp, N_HD), seq), pl.BlockSpec((1, 1, nbp, N_HD), seq),
                  pl.BlockSpec((1, 1, T, N_HD), seq), pl.BlockSpec((1, 1, T, N_HD), seq),
                  pl.BlockSpec((1, 1, T, N_HD), seq), pl.BlockSpec((1, 1, T, N_HD), seq)],
        out_specs=pl.BlockSpec((Q_BLOCK, KV_W), lambda b, kv, i: (b * nq + i, kv)),
        out_shape=jax.ShapeDtypeStruct((B * T, N_HEADS * N_HD), F32),
        compiler_params=_cparams(("parallel", "parallel", "arbitrary")),
        name="nsa_prompt",
    )(z, zs, kc, vc, ks, vs, kw, vw)


def _nsa_sample_kernel(pt_ref, q_ref, zs_ref, kc_ref, vc_ref, ksn_ref, vsn_ref, kwn_ref, vwn_ref, wk_ref, wv_ref,
                       pk_ref, pv_ref, o_ref, kbuf, vbuf, sem_k, sem_v, *, past, n_new, nbc, n_sel, wp):
    b = pl.program_id(0)
    n_pages = past // PAGE_SIZE
    tile = SAMPLE_PAD
    rows = N_GROUP * tile

    def start(p, c):
        _page_copy(pt_ref, b, p, pk_ref, kbuf, sem_k, PAGE_SIZE).start()
        _page_copy(pt_ref, b, p, pv_ref, vbuf, sem_v, PAGE_SIZE).start()
        return c

    def wait(p, c):
        _page_copy(pt_ref, b, p, pk_ref, kbuf, sem_k, PAGE_SIZE).wait()
        _page_copy(pt_ref, b, p, pv_ref, vbuf, sem_v, PAGE_SIZE).wait()
        return c

    lax.fori_loop(0, n_pages, start, 0)

    trow = lax.broadcasted_iota(jnp.int32, (rows, 1), 0) % tile
    tq = past + trow
    tq_t = past + lax.broadcasted_iota(jnp.int32, (tile, 1), 0)
    gates = _sigmoid(zs_ref[...])
    q = q_ref[...]
    newpos = past + lax.broadcasted_iota(jnp.int32, (1, tile), 1)
    new_ok = (newpos < past + n_new) & (newpos <= tq)
    wpos = past - WINDOW + lax.broadcasted_iota(jnp.int32, (1, WINDOW), 1)
    wdist = tq - wpos
    win_ok = (wdist >= 0) & (wdist < WINDOW) & (wpos >= 0)

    pre = []
    for kv in range(N_KV):
        hs = slice(kv * N_HD, (kv + 1) * N_HD)
        qs = _stack_heads(q[:, kv * KV_W:(kv + 1) * KV_W], tile)
        o_cmp, imp = _compressed_branch(qs, kc_ref[0, kv], vc_ref[0, kv], tq, nbc, wp, tile)
        sel = _select_blocks(imp, tq_t, n_sel, wp)
        s1 = jnp.where(win_ok, _mm_nt(qs, wk_ref[0, :, hs]) * SCALE, NEG)
        s2 = jnp.where(new_ok, _mm_nt(qs, kwn_ref[:, hs]) * SCALE, NEG)
        m = jnp.maximum(jnp.max(s1, axis=1, keepdims=True), jnp.max(s2, axis=1, keepdims=True))
        e1 = jnp.where(win_ok, jnp.exp(s1 - m), 0.0)
        e2 = jnp.where(new_ok, jnp.exp(s2 - m), 0.0)
        den = jnp.maximum(jnp.sum(e1, axis=1, keepdims=True) + jnp.sum(e2, axis=1, keepdims=True), 1.0)
        o_win = (_mm(e1, wv_ref[0, :, hs]) + _mm(e2, vwn_ref[:, hs])) / den
        pre.append((qs, o_cmp, sel, o_win))

    lax.fori_loop(0, n_pages, wait, 0)

    outs = []
    for kv in range(N_KV):
        hs = slice(kv * N_HD, (kv + 1) * N_HD)
        qs, o_cmp, sel, o_win = pre[kv]

        def slc_step(j, carry):
            k0 = pl.multiple_of(j * SLC_TILE, SLC_TILE)
            kpos = k0 + lax.broadcasted_iota(jnp.int32, (1, SLC_TILE), 1)
            s = _mm_nt(qs, kbuf[pl.ds(k0, SLC_TILE), hs]) * SCALE
            mt = (_expand_sel(sel, kpos, wp) > 0.5) & (kpos <= tq_t)
            mask = jnp.concatenate([mt] * N_GROUP, axis=0)
            return _online_step(carry, s, mask, vbuf[pl.ds(k0, SLC_TILE), hs])

        init = (jnp.full((rows, 1), NEG, F32), jnp.zeros((rows, 1), F32), jnp.zeros((rows, N_HD), F32))
        carry = lax.fori_loop(0, past // SLC_TILE, slc_step, init)
        s_new = _mm_nt(qs, ksn_ref[:, hs]) * SCALE
        mt = (_expand_sel(sel, newpos, wp) > 0.5)
        mask = jnp.concatenate([mt] * N_GROUP, axis=0) & new_ok
        _, l, acc = _online_step(carry, s_new, mask, vsn_ref[:, hs])
        o_slc = acc / jnp.maximum(l, 1.0)
        outs.append(_combine(gates, kv * N_GROUP, o_cmp, o_slc, o_win, tile))
    o_ref[...] = jnp.concatenate(outs, axis=1)


def _nsa_sample(page_table, z, zs, row_blk0, kc, vc, win_k, win_v, pool_k, pool_v, n_new, nbc, n_sel, wp):
    BS, n_pages = page_table.shape
    past = n_pages * PAGE_SIZE
    kern = functools.partial(_nsa_sample_kernel, past=past, n_new=n_new, nbc=nbc, n_sel=n_sel, wp=wp)
    new = lambda off: pl.BlockSpec((SAMPLE_PAD, KV_W), lambda b, pt: (row_blk0 + b, off // KV_W))
    seq4 = lambda b, pt: (b, 0, 0, 0)
    gs = pltpu.PrefetchScalarGridSpec(
        num_scalar_prefetch=1,
        grid=(BS,),
        in_specs=[pl.BlockSpec((SAMPLE_PAD, N_HEADS * N_HD), lambda b, pt: (row_blk0 + b, OFF_NQ // (N_HEADS * N_HD))),
                  pl.BlockSpec((SAMPLE_PAD, SMALL_W), lambda b, pt: (row_blk0 + b, 0)),
                  pl.BlockSpec((1, N_KV, kc.shape[2], N_HD), seq4), pl.BlockSpec((1, N_KV, kc.shape[2], N_HD), seq4),
                  new(OFF_KS), new(OFF_VS), new(OFF_KW), new(OFF_VW),
                  pl.BlockSpec((1, WINDOW, KV_W), lambda b, pt: (b, 0, 0)),
                  pl.BlockSpec((1, WINDOW, KV_W), lambda b, pt: (b, 0, 0)),
                  pl.BlockSpec(memory_space=pl.ANY), pl.BlockSpec(memory_space=pl.ANY)],
        out_specs=pl.BlockSpec((SAMPLE_PAD, N_HEADS * N_HD), lambda b, pt: (b, 0)),
        scratch_shapes=[pltpu.VMEM((past, KV_W), F32), pltpu.VMEM((past, KV_W), F32),
                        pltpu.SemaphoreType.DMA(()), pltpu.SemaphoreType.DMA(())],
    )
    return pl.pallas_call(
        kern,
        grid_spec=gs,
        out_shape=jax.ShapeDtypeStruct((BS * SAMPLE_PAD, N_HEADS * N_HD), F32),
        compiler_params=_cparams(("arbitrary",)),
        name="nsa_sample",
    )(page_table, z, zs, kc, vc, z, z, z, z, win_k, win_v, pool_k, pool_v)


MERGE_TILE = 256
P_QW = P_HEADS * P_DKEY
P_EXPERTS = P_NKEYS * P_NKEYS
P_HK = P_HEADS * P_TOPK


def _merge_kernel(x_ref, ha_ref, hb_ref, ga0_ref, ga1_ref, gb0_ref, gb1_ref, wo_ref, gain_ref, wq_ref,
                  x1_ref, xn_ref, qp_ref):
    ga = _sigmoid(jnp.concatenate([ga0_ref[...], ga1_ref[...]], axis=1))
    gb = _sigmoid(jnp.concatenate([gb0_ref[...], gb1_ref[...]], axis=1))
    x1 = x_ref[...] + _mm(ga * ha_ref[...] + gb * hb_ref[...], wo_ref[...])
    x1_ref[...] = x1
    xn = (x1 * lax.rsqrt(jnp.mean(x1 * x1, axis=-1, keepdims=True) + EPS) * gain_ref[...]).astype(MXU_DTYPE)
    xn_ref[...] = xn
    qp_ref[...] = jnp.dot(xn, wq_ref[...], preferred_element_type=F32)


def _merge(x_all, z, ha, hb, row0, w_out, gain, wq):
    n = ha.shape[0]
    tm = MERGE_TILE
    assert n % tm == 0 and row0 % tm == 0
    r0 = row0 // tm
    zc = lambda off: pl.BlockSpec((tm, COL_TILE), lambda i: (r0 + i, off // COL_TILE))
    full = lambda shp: pl.BlockSpec(shp, lambda i: (0, 0))
    loc = lambda w: pl.BlockSpec((tm, w), lambda i: (i, 0))
    return pl.pallas_call(
        _merge_kernel,
        grid=(n // tm,),
        in_specs=[pl.BlockSpec((tm, D_MODEL), lambda i: (r0 + i, 0)), loc(D_MODEL), loc(D_MODEL),
                  zc(OFF_GA), zc(OFF_GA + COL_TILE), zc(OFF_GB), zc(OFF_GB + COL_TILE),
                  full((D_MODEL, D_MODEL)), full((1, D_MODEL)), full((D_MODEL, P_QW))],
        out_specs=[loc(D_MODEL), loc(D_MODEL), loc(P_QW)],
        out_shape=[jax.ShapeDtypeStruct((n, D_MODEL), F32), jax.ShapeDtypeStruct((n, D_MODEL), MXU_DTYPE),
                   jax.ShapeDtypeStruct((n, P_QW), F32)],
        compiler_params=_cparams(("parallel",)),
        name="merge",
    )(x_all, ha, hb, z, z, z, z, w_out, gain, wq)


PEER_TILE = 256
PEER_EB = 1024
PEER_SLABS = PEER_EB // P_NKEYS


def _extract_topk(s, k, put):
    rows, width = s.shape
    lane = lax.broadcasted_iota(jnp.int32, (rows, width), 1).astype(F32)

    def body(r, carry):
        s, state = carry
        m = jnp.max(s, axis=1, keepdims=True)
        idx = jnp.min(jnp.where(s == m, lane, float(width)), axis=1, keepdims=True)
        return jnp.where(lane == idx, -INF, s), put(r, m, idx, state)

    return body


def _peer_select(qp_ref, sk_ref, a_scr, b_scr, g_scr):
    tn = qp_ref.shape[0]
    nc = P_TOPK * P_TOPK
    lane_c = lax.broadcasted_iota(jnp.int32, (tn, nc), 1)
    lane_o = lax.broadcasted_iota(jnp.int32, (tn, P_HK), 1)
    zeros_c = jnp.zeros((tn, nc), F32)
    e_all = jnp.zeros((tn, P_HK), F32)
    best_all = jnp.zeros((tn, P_HK), F32)
    top_all = jnp.zeros((tn, P_HK), F32)
    for h in range(P_HEADS):
        halves = []
        for c in range(2):
            hc = 2 * h + c
            s = _mm_nt(qp_ref[:, hc * P_NKEYS:(hc + 1) * P_NKEYS], sk_ref[hc])
            slot = (lane_c // P_TOPK) if c == 0 else (lane_c % P_TOPK)

            def put(r, m, idx, state, slot=slot):
                val, ind = state
                return jnp.where(slot == r, m, val), jnp.where(slot == r, idx, ind)

            _, (val, ind) = lax.fori_loop(0, P_TOPK, _extract_topk(s, P_TOPK, put), (s, (zeros_c, zeros_c)))
            halves.append((val, ind))
        cand = halves[0][0] + halves[1][0]
        cexp = halves[0][1] * float(P_NKEYS) + halves[1][1]

        def put2(r, m, idx, state, h=h, cexp=cexp):
            e_acc, b_acc, lane_hit = state
            hit = lax.broadcasted_iota(jnp.int32, (tn, nc), 1).astype(F32) == idx
            e = jnp.sum(jnp.where(hit, cexp, 0.0), axis=1, keepdims=True)
            o = lane_o == h * P_TOPK + r
            return jnp.where(o, e, e_acc), jnp.where(o, m, b_acc), lane_hit

        _, (e_all, best_all, _) = lax.fori_loop(0, P_TOPK, _extract_topk(cand, P_TOPK, put2),
                                                (cand, (e_all, best_all, 0)))
        top_all = jnp.where(lane_o // P_TOPK == h, jnp.max(cand, axis=1, keepdims=True), top_all)
    ex = jnp.exp(best_all - top_all)
    den = jnp.zeros((tn, P_HK), F32)
    for h in range(P_HEADS):
        seg = lane_o // P_TOPK == h
        den = jnp.where(seg, jnp.sum(jnp.where(seg, ex, 0.0), axis=1, keepdims=True), den)
    a = jnp.floor(e_all * (1.0 / P_NKEYS))
    a_scr[...] = a
    b_scr[...] = e_all - a * float(P_NKEYS)
    g_scr[...] = ex / den


def _peer_kernel(x1_ref, xn_ref, qp_ref, sk_ref, u_ref, v_ref, gain_ref, y_ref,
                 a_scr, b_scr, g_scr, w_scr, acc_scr):
    j = pl.program_id(1)
    tn = x1_ref.shape[0]

    @pl.when(j == 0)
    def _():
        _peer_select(qp_ref, sk_ref, a_scr, b_scr, g_scr)
        sub = lax.broadcasted_iota(jnp.int32, (P_NKEYS, P_HK), 0).astype(F32)

        def per_token(n, c):
            a_row = a_scr[pl.ds(n, 1), :]
            b_row = b_scr[pl.ds(n, 1), :]
            g_row = g_scr[pl.ds(n, 1), :]
            at = jnp.where(sub == a_row, g_row, 0.0)
            bt = jnp.where(sub == b_row, 1.0, 0.0)
            w_scr[pl.ds(pl.multiple_of(n * P_NKEYS, P_NKEYS), P_NKEYS), :] = _mm_nt(at, bt)
            return c

        lax.fori_loop(0, tn, per_token, 0)
        acc_scr[...] = jnp.zeros(acc_scr.shape, F32)

    act = _gelu(_mm_nt(xn_ref[...], u_ref[...]))
    parts = []
    for r in range(PEER_SLABS):
        w = w_scr[pl.ds(j * PEER_SLABS + r, tn, stride=P_NKEYS), :]
        parts.append(w * act[:, r * P_NKEYS:(r + 1) * P_NKEYS])
    acc_scr[...] += _mm(jnp.concatenate(parts, axis=1), v_ref[...])

    @pl.when(j == pl.num_programs(1) - 1)
    def _():
        x2 = x1_ref[...] + acc_scr[...]
        y_ref[...] = x2 * lax.rsqrt(jnp.mean(x2 * x2, axis=-1, keepdims=True) + EPS) * gain_ref[...]


def _peer(x1, xn, qp, subkeys, u, v, gain):
    n = x1.shape[0]
    tn = PEER_TILE
    assert n % tn == 0
    tok = lambda w: pl.BlockSpec((tn, w), lambda i, j: (i, 0))
    return pl.pallas_call(
        _peer_kernel,
        grid=(n // tn, P_EXPERTS // PEER_EB),
        in_specs=[tok(D_MODEL), tok(D_MODEL), tok(P_QW),
                  pl.BlockSpec((2 * P_HEADS, P_NKEYS, P_DKEY // 2), lambda i, j: (0, 0, 0)),
                  pl.BlockSpec((PEER_EB, D_MODEL), lambda i, j: (j, 0)),
                  pl.BlockSpec((PEER_EB, D_MODEL), lambda i, j: (j, 0)),
                  pl.BlockSpec((1, D_MODEL), lambda i, j: (0, 0))],
        out_specs=tok(D_MODEL),
        out_shape=jax.ShapeDtypeStruct((n, D_MODEL), F32),
        scratch_shapes=[pltpu.VMEM((tn, P_HK), F32), pltpu.VMEM((tn, P_HK), F32), pltpu.VMEM((tn, P_HK), F32),
                        pltpu.VMEM((tn * P_NKEYS, P_NKEYS), F32), pltpu.VMEM((tn, D_MODEL), F32)],
        compiler_params=_cparams(("parallel", "arbitrary")),
        name="peer",
    )(x1, xn, qp, subkeys, u, v, gain)


def _split_w_in(w):
    widths = [2 * M_HEADS * M_DQK, M_HEADS * M_DV, M_HEADS, M_HEADS, M_HEADS * M_DV, N_HEADS * N_HD] + \
             [N_KV * N_HD] * 6 + [3 * N_HEADS, D_MODEL, D_MODEL]
    names = ['m_qk', 'm_v', 'm_i', 'm_f', 'm_o', 'n_q', 'n_kc', 'n_vc', 'n_ks', 'n_vs', 'n_kw', 'n_vw', 'n_g', 'g_a', 'g_b']
    out, off = {}, 0
    for nme, wd in zip(names, widths):
        out[nme] = w[:, off:off + wd]
        off += wd
    return out


def _stage_a(x_prompt, x_sample, past, norm_mix, w_in):
    B, T, D = x_prompt.shape
    BS, TS, _ = x_sample.shape
    xs_pad = jnp.pad(x_sample, ((0, 0), (0, SAMPLE_PAD - TS), (0, 0)))
    x_all = jnp.concatenate([x_prompt.reshape(B * T, D), xs_pad.reshape(BS * SAMPLE_PAD, D)], axis=0)
    rows = x_all.shape[0]
    rows_pad = -(-rows // ROW_TILE) * ROW_TILE
    x_all = jnp.pad(x_all, ((0, rows_pad - rows), (0, 0)))
    pos = jnp.concatenate([jnp.tile(jnp.arange(T, dtype=jnp.int32), B),
                           jnp.tile(past + jnp.arange(SAMPLE_PAD, dtype=jnp.int32), BS),
                           jnp.zeros((rows_pad - rows,), jnp.int32)])
    cos_t, sin_t = _rope_tables(pos)
    cols = _split_w_in(w_in)
    w_main = jnp.concatenate([cols[k] for k in ('m_qk', 'm_v', 'm_o', 'n_q', 'n_kc', 'n_vc', 'n_ks', 'n_vs',
                                                'n_kw', 'n_vw', 'g_a', 'g_b')], axis=1).astype(MXU_DTYPE)
    w_small = jnp.concatenate([cols['m_i'], cols['m_f'], cols['n_g'],
                               jnp.zeros((D, SMALL_W - 2 * M_HEADS - 3 * N_HEADS), w_in.dtype)], axis=1).astype(MXU_DTYPE)
    z, zs = _inproj(x_all, norm_mix.reshape(1, D), w_main, w_small, cos_t, sin_t)
    return x_all, z, zs


def kernel(x_prompt, x_sample, cache_cmp_k, cache_cmp_v, cache_slc_k, cache_slc_v, state_win_k, state_win_v, state_conv, state_C, state_n, state_m, page_table, norm_mix, w_in, m_conv_w, m_conv_b, m_bias_i, m_bias_f, m_norm, cmp_k_w1, cmp_k_b1, cmp_k_w2, cmp_v_w1, cmp_v_b1, cmp_v_w2, w_out, norm_ffn, peer_wq, peer_subkeys, peer_u, peer_v, norm_final):
    B, T, D = x_prompt.shape
    BS, TS, _ = x_sample.shape
    past = page_table.shape[1] * PAGE_SIZE
    assert TS <= SAMPLE_PAD and past >= WINDOW and T >= WIN_SPAN and T % ROW_TILE == 0
    n_rows_p, n_rows_s = B * T, BS * SAMPLE_PAD
    x_all, z, zs = _stage_a(x_prompt, x_sample, past, norm_mix[0], w_in[0])
    zeros = lambda *s: jnp.zeros(s, F32)

    brow = jnp.concatenate([m_bias_i[0], m_bias_f[0], zeros(SMALL_W - 2 * M_HEADS)]).reshape(1, -1)
    norm = m_norm[0].reshape(1, -1)
    cb = m_conv_b[0].reshape(1, -1)
    qk_w = 2 * M_HEADS * M_DQK
    ha_p, C_p, n_p, m_p = _mlstm(z, zs, 0, B, T, ROW_TILE, M_CHUNK, M_CHUNK, zeros(B, 8, qk_w), m_conv_w[0], cb, brow,
                                 norm, zeros(B, M_HEADS, M_DQK, M_DV), zeros(B, M_HEADS, 1, M_DQK),
                                 zeros(B, M_HEADS, 1, 128))
    hist = jnp.pad(state_conv[0], ((0, 0), (8 - (M_CONV - 1), 0), (0, 0)))
    m0 = jnp.broadcast_to(state_m[0][:, :, None, None], (BS, M_HEADS, 1, 128))
    ha_s, C_s, n_s, m_s = _mlstm(z, zs, n_rows_p, BS, SAMPLE_PAD, SAMPLE_PAD, SAMPLE_PAD, TS, hist, m_conv_w[0], cb,
                                 brow, norm, state_C[0], state_n[0][:, :, None, :], m0)

    sect = lambda off, r0, r1: z[r0:r1, off:off + KV_W]
    wk = _compress_weights(cmp_k_w1[0], cmp_k_b1[0], cmp_k_w2[0])
    wv = _compress_weights(cmp_v_w1[0], cmp_v_b1[0], cmp_v_w2[0])
    up128 = lambda n: -(-n // 128) * 128
    nbc_p = T // CMP_BLOCK
    n_sel_p = T // SLC_BLOCK
    kcmp_p, vcmp_p = sect(OFF_KC, 0, n_rows_p), sect(OFF_VC, 0, n_rows_p)
    kc_p = _compress_prompt(z, B, T, OFF_KC, wk, _block_tables(nbc_p, True))
    vc_p = _compress_prompt(z, B, T, OFF_VC, wv, _block_tables(nbc_p, False))
    prompt_sec = {off: sect(off, 0, n_rows_p) for off in (OFF_KS, OFF_VS, OFF_KW, OFF_VW)}
    heads_first = lambda a: a.reshape(B, T, N_KV, N_HD).transpose(0, 2, 1, 3).astype(MXU_DTYPE)
    hb_p = _nsa_prompt(z, zs, kc_p, vc_p, *[heads_first(prompt_sec[o]) for o in (OFF_KS, OFF_VS, OFF_KW, OFF_VW)],
                       B, T, nbc_p, n_sel_p, up128(n_sel_p))

    nb_past = past // CMP_BLOCK
    nbc_s = nb_past + SLC_BLOCK // CMP_BLOCK
    n_sel_s = nbc_s * CMP_BLOCK // SLC_BLOCK
    nbp_s = up128(nb_past + 8)
    row_blk_s = n_rows_p // SAMPLE_PAD
    raw_pool = lambda pool: pool[0].reshape(-1, PAGE_SIZE, KV_W)
    kc_s = _compress_sample(page_table, raw_pool(cache_cmp_k), z, row_blk_s, OFF_KC // KV_W, wk,
                            _block_tables(nbp_s, True), nbp_s, TS)
    vc_s = _compress_sample(page_table, raw_pool(cache_cmp_v), z, row_blk_s, OFF_VC // KV_W, wv,
                            _block_tables(nbp_s, False), nbp_s, TS)
    n_win = state_win_k.shape[2]
    hb_s = _nsa_sample(page_table, z, zs, row_blk_s, kc_s, vc_s, state_win_k[0].reshape(BS, n_win, KV_W),
                       state_win_v[0].reshape(BS, n_win, KV_W), raw_pool(cache_slc_k), raw_pool(cache_slc_v),
                       TS, nbc_s, n_sel_s, up128(n_sel_s))

    wo = w_out[0].astype(MXU_DTYPE)
    wq = peer_wq[0].astype(MXU_DTYPE)
    gain_ffn = norm_ffn[0].reshape(1, D)
    sk = peer_subkeys[0].reshape(2 * P_HEADS, P_NKEYS, P_DKEY // 2).astype(MXU_DTYPE)
    pu, pv = peer_u[0].astype(MXU_DTYPE), peer_v[0].astype(MXU_DTYPE)
    gain_out = norm_final.reshape(1, D)
    ys = []
    for ha, hb, row0 in ((ha_p, hb_p, 0), (ha_s, hb_s, n_rows_p)):
        x1, xn, qp = _merge(x_all, z, ha, hb, row0, wo, gain_ffn, wq)
        ys.append(_peer(x1, xn, qp, sk, pu, pv, gain_out))
    y_prompt = ys[0].reshape(B, T, D)
    y_sample = ys[1].reshape(BS, SAMPLE_PAD, D)[:, :TS]

    kvshape_p = (1, B, T, N_KV, N_HD)
    zsam = z[n_rows_p:n_rows_p + n_rows_s].reshape(BS, SAMPLE_PAD, Z_MAIN)[:, :TS]
    new_s = lambda off: zsam[:, :, off:off + KV_W].reshape(1, BS, TS, N_KV, N_HD)
    n_win_p = min(WINDOW, T)
    win_p = lambda off: prompt_sec[off].reshape(kvshape_p)[:, :, T - n_win_p:]
    win_s = lambda state, off: jnp.concatenate([state, new_s(off)], axis=2)[:, :, -n_win:]
    conv_p = z[:n_rows_p, OFF_MQ:OFF_MQ + qk_w].reshape(1, B, T, qk_w)[:, :, T - (M_CONV - 1):]
    conv_s = jnp.concatenate([state_conv, zsam[None, :, :, OFF_MQ:OFF_MQ + qk_w]], axis=2)[:, :, -(M_CONV - 1):]
    return (y_prompt, y_sample,
            kcmp_p.reshape(kvshape_p), new_s(OFF_KC), vcmp_p.reshape(kvshape_p), new_s(OFF_VC),
            prompt_sec[OFF_KS].reshape(kvshape_p), new_s(OFF_KS), prompt_sec[OFF_VS].reshape(kvshape_p), new_s(OFF_VS),
            win_p(OFF_KW), win_s(state_win_k, OFF_KW), win_p(OFF_VW), win_s(state_win_v, OFF_VW),
            conv_p, conv_s, C_p[None], C_s[None], n_p[None, :, :, 0], n_s[None, :, :, 0],
            m_p[None, :, :, 0, 0], m_s[None, :, :, 0, 0])
```

```python
import functools

import jax
import jax.numpy as jnp
from jax import lax
from jax.experimental import pallas as pl
from jax.experimental.pallas import tpu as pltpu

F32 = jnp.float32
MXU_DTYPE = jnp.bfloat16
HIGHEST = lax.Precision.HIGHEST
EPS = 1e-6
NEG = -1e30

D_MODEL = 1024
PAGE_SIZE = 128
M_HEADS, M_DQK, M_DV, M_CONV, M_CHUNK = 4, 128, 256, 4, 64
N_HEADS, N_KV, N_GROUP, N_HD = 16, 4, 4, 64
CMP_BLOCK, CMP_HIDDEN, SLC_BLOCK, SLC_TOPK, WINDOW, Q_BLOCK = 32, 128, 64, 16, 512, 128
ROPE_THETA, ROPE_DIM = 500000.0, 16
P_HEADS, P_NKEYS, P_DKEY, P_TOPK = 8, 128, 256, 16

OFF_MQ, OFF_MK, OFF_MV, OFF_MO, OFF_NQ = 0, 512, 1024, 2048, 3072
OFF_KC, OFF_VC, OFF_KS, OFF_VS, OFF_KW, OFF_VW = 4096, 4352, 4608, 4864, 5120, 5376
OFF_GA, OFF_GB, Z_MAIN = 5632, 6656, 7680
SMALL_W = 128
SM_I, SM_F, SM_G = 0, 4, 8
ROW_TILE = 512
COL_TILE = 512
SAMPLE_PAD = 8
VMEM_LIMIT = 56 * 1024 * 1024


def _cparams(sem):
    return pltpu.CompilerParams(dimension_semantics=sem, vmem_limit_bytes=VMEM_LIMIT)


def _mm(a, b):
    return jnp.dot(a.astype(MXU_DTYPE), b.astype(MXU_DTYPE), preferred_element_type=F32)


def _mm_nt(a, b):
    return lax.dot_general(a.astype(MXU_DTYPE), b.astype(MXU_DTYPE), (((1,), (1,)), ((), ())),
                           preferred_element_type=F32)


def _mm_tn(a, b):
    return lax.dot_general(a.astype(MXU_DTYPE), b.astype(MXU_DTYPE), (((0,), (0,)), ((), ())),
                           preferred_element_type=F32)


def _log_sigmoid(x):
    return jnp.minimum(x, 0.0) - jnp.log(1.0 + jnp.exp(-jnp.abs(x)))


def _sigmoid(x):
    return 1.0 / (1.0 + jnp.exp(-x))


def _col_to_row(x_col):
    L = x_col.shape[0]
    ti = lax.broadcasted_iota(jnp.int32, (L, L), 0)
    si = lax.broadcasted_iota(jnp.int32, (L, L), 1)
    return jnp.sum(jnp.where(ti == si, x_col, 0.0), axis=0, keepdims=True)


def _inproj_kernel(x_ref, g_ref, w_ref, ws_ref, cos_ref, sin_ref, z_ref, zs_ref, xn_scr):
    j = pl.program_id(1)

    @pl.when(j == 0)
    def _():
        x = x_ref[...]
        r = x * lax.rsqrt(jnp.mean(x * x, axis=-1, keepdims=True) + EPS) * g_ref[...]
        xn = r.astype(MXU_DTYPE)
        xn_scr[...] = xn
        zs_ref[...] = jnp.dot(xn, ws_ref[...], preferred_element_type=F32)

    acc = jnp.dot(xn_scr[...], w_ref[...], preferred_element_type=F32)
    rope_full = (j == OFF_NQ // COL_TILE) | (j == OFF_NQ // COL_TILE + 1)
    rope_half = (j == OFF_KS // COL_TILE) | (j == OFF_KW // COL_TILE)
    is_rope = rope_full | rope_half

    @pl.when(is_rope)
    def _():
        lane = lax.broadcasted_iota(jnp.int32, acc.shape, 1)
        on = lane < jnp.where(rope_full, COL_TILE, COL_TILE // 2)
        reps = COL_TILE // cos_ref.shape[1]
        cos = jnp.where(on, jnp.concatenate([cos_ref[...]] * reps, axis=1), 1.0)
        sin = jnp.where(on, jnp.concatenate([sin_ref[...]] * reps, axis=1), 0.0)
        half = ROPE_DIM // 2
        partner = jnp.where(lane % N_HD < half, pltpu.roll(acc, COL_TILE - half, 1), pltpu.roll(acc, half, 1))
        z_ref[...] = acc * cos + partner * sin

    @pl.when(jnp.logical_not(is_rope))
    def _():
        z_ref[...] = acc


def _inproj(x_all, gain, w_main, w_small, cos_t, sin_t):
    rows = x_all.shape[0]
    grid = (rows // ROW_TILE, Z_MAIN // COL_TILE)
    return pl.pallas_call(
        _inproj_kernel,
        grid=grid,
        in_specs=[
            pl.BlockSpec((ROW_TILE, D_MODEL), lambda i, j: (i, 0)),
            pl.BlockSpec((1, D_MODEL), lambda i, j: (0, 0)),
            pl.BlockSpec((D_MODEL, COL_TILE), lambda i, j: (0, j)),
            pl.BlockSpec((D_MODEL, SMALL_W), lambda i, j: (0, 0)),
            pl.BlockSpec((ROW_TILE, 128), lambda i, j: (i, 0)),
            pl.BlockSpec((ROW_TILE, 128), lambda i, j: (i, 0)),
        ],
        out_specs=[
            pl.BlockSpec((ROW_TILE, COL_TILE), lambda i, j: (i, j)),
            pl.BlockSpec((ROW_TILE, SMALL_W), lambda i, j: (i, 0)),
        ],
        out_shape=[
            jax.ShapeDtypeStruct((rows, Z_MAIN), F32),
            jax.ShapeDtypeStruct((rows, SMALL_W), F32),
        ],
        scratch_shapes=[pltpu.VMEM((ROW_TILE, D_MODEL), MXU_DTYPE)],
        compiler_params=_cparams(("parallel", "arbitrary")),
        name="inproj",
    )(x_all, gain, w_main, w_small, cos_t, sin_t)


def _rope_tables(pos):
    half = ROPE_DIM // 2
    inv = ROPE_THETA ** (-jnp.arange(half, dtype=F32) / half)
    ang = pos.astype(F32)[:, None] * inv
    cos, sin = jnp.cos(ang), jnp.sin(ang)
    rows = pos.shape[0]
    pad1 = jnp.ones((rows, N_HD - ROPE_DIM), F32)
    pad0 = jnp.zeros((rows, N_HD - ROPE_DIM), F32)
    cos_h = jnp.concatenate([cos, cos, pad1], axis=1)
    sin_h = jnp.concatenate([-sin, sin, pad0], axis=1)
    return jnp.concatenate([cos_h, cos_h], axis=1), jnp.concatenate([sin_h, sin_h], axis=1)


def _mlstm_kernel(zq_ref, zk_ref, zv_ref, zo_ref, zs_ref, hq_ref, hk_ref, cwq_ref, cwk_ref, cbq_ref, cbk_ref,
                  brow_ref, norm_ref, c0_ref, n0_ref, m0_ref,
                  h_ref, c_ref, n_ref, m_ref, c_scr, n_scr, m_scr, hq_scr, hk_scr, *, chunk, nvalid):
    t = pl.program_id(2)
    h = pl.program_id(1)
    rb = zq_ref.shape[0]

    @pl.when(t == 0)
    def _():
        c_scr[...] = c0_ref[0, 0]
        n_scr[...] = n0_ref[0, 0]
        m_scr[...] = m0_ref[0, 0]
        hq_scr[...] = hq_ref[0]
        hk_scr[...] = hk_ref[0]

    def conv_silu(z_ref_, hist_scr, cw_ref, cb_ref):
        z = z_ref_[...]
        ext = jnp.concatenate([hist_scr[...], z], axis=0)
        acc = cb_ref[...]
        for jj in range(M_CONV):
            off = 8 - (M_CONV - 1) + jj
            acc = acc + ext[off:off + rb] * cw_ref[jj:jj + 1, :]
        hist_scr[...] = z[rb - 8:rb]
        return acc * _sigmoid(acc)

    q_all = conv_silu(zq_ref, hq_scr, cwq_ref, cbq_ref)
    k_all = conv_silu(zk_ref, hk_scr, cwk_ref, cbk_ref) * (M_DQK ** -0.5)

    zs = zs_ref[...] + brow_ref[...]
    col = lax.broadcasted_iota(jnp.int32, (rb, SMALL_W), 1)
    li_cols = jnp.sum(jnp.where(col == SM_I + h, zs, 0.0), axis=1, keepdims=True)
    lf_cols = _log_sigmoid(jnp.sum(jnp.where(col == SM_F + h, zs, 0.0), axis=1, keepdims=True))

    L = chunk
    ti = lax.broadcasted_iota(jnp.int32, (L, L), 0)
    si = lax.broadcasted_iota(jnp.int32, (L, L), 1)
    mask = (si <= ti) & (si < nvalid)
    tril = (si <= ti).astype(F32)
    tcol = lax.broadcasted_iota(jnp.int32, (L, 1), 0)

    C = c_scr[...]
    n = n_scr[...]
    m = m_scr[:, 0:1]
    gain = norm_ref[...]
    for c in range(rb // L):
        sl = slice(c * L, (c + 1) * L)
        q, k, v = q_all[sl], k_all[sl], zv_ref[sl, :]
        li_c = li_cols[sl]
        lf_c = jnp.where(tcol < nvalid, lf_cols[sl], 0.0)
        b_c = jnp.dot(tril, jnp.broadcast_to(lf_c, (L, 128)), precision=HIGHEST, preferred_element_type=F32)[:, 0:1]
        b_r = _col_to_row(b_c)
        li_r = _col_to_row(li_c)
        logw = jnp.where(mask, b_c - b_r + li_r, NEG)
        inter = b_c + m
        m_t = jnp.maximum(inter, jnp.max(logw, axis=1, keepdims=True))
        w = jnp.where(mask, jnp.exp(logw - m_t), 0.0) * _mm_nt(q, k)
        w_inter = jnp.exp(inter - m_t)
        num = _mm(w, v) + w_inter * _mm(q, C)
        den = jnp.sum(w, axis=1, keepdims=True) + w_inter * jnp.sum(q * n, axis=1, keepdims=True)
        hh = num / jnp.maximum(jnp.abs(den), jnp.exp(-m_t))
        b_end = b_c[L - 1:L]
        m_new = m_t[L - 1:L]
        w_end = jnp.where(tcol < nvalid, jnp.exp(b_end - b_c + li_c - m_new), 0.0)
        decay = jnp.exp(b_end + m - m_new)
        kw = w_end * k
        C = decay * C + _mm_tn(kw, v)
        n = decay * n + jnp.sum(kw, axis=0, keepdims=True)
        m = m_new
        hh = hh * lax.rsqrt(jnp.mean(hh * hh, axis=-1, keepdims=True) + EPS) * gain
        h_ref[sl, :] = _sigmoid(zo_ref[sl, :]) * hh

    c_scr[...] = C
    n_scr[...] = n
    m_scr[...] = jnp.broadcast_to(m, m_scr.shape)

    @pl.when(t == pl.num_programs(2) - 1)
    def _():
        c_ref[0, 0] = C
        n_ref[0, 0] = n
        m_ref[0, 0] = jnp.broadcast_to(m, m_scr.shape)


def _mlstm(z, zs, row0, nb, rows_per_seq, rb, chunk, nvalid, hist, conv_w, conv_b, brow, norm, c0, n0, m0):
    assert row0 % rb == 0 and rows_per_seq % rb == 0 and rb % chunk == 0
    nt = rows_per_seq // rb
    rblk = lambda b, t: row0 // rb + b * nt + t
    kern = functools.partial(_mlstm_kernel, chunk=chunk, nvalid=nvalid)
    return pl.pallas_call(
        kern,
        grid=(nb, M_HEADS, nt),
        in_specs=[
            pl.BlockSpec((rb, M_DQK), lambda b, h, t: (rblk(b, t), OFF_MQ // M_DQK + h)),
            pl.BlockSpec((rb, M_DQK), lambda b, h, t: (rblk(b, t), OFF_MK // M_DQK + h)),
            pl.BlockSpec((rb, M_DV), lambda b, h, t: (rblk(b, t), OFF_MV // M_DV + h)),
            pl.BlockSpec((rb, M_DV), lambda b, h, t: (rblk(b, t), OFF_MO // M_DV + h)),
            pl.BlockSpec((rb, SMALL_W), lambda b, h, t: (rblk(b, t), 0)),
            pl.BlockSpec((1, 8, M_DQK), lambda b, h, t: (b, 0, OFF_MQ // M_DQK + h)),
            pl.BlockSpec((1, 8, M_DQK), lambda b, h, t: (b, 0, OFF_MK // M_DQK + h)),
            pl.BlockSpec((M_CONV, M_DQK), lambda b, h, t: (0, h)),
            pl.BlockSpec((M_CONV, M_DQK), lambda b, h, t: (0, M_HEADS + h)),
            pl.BlockSpec((1, M_DQK), lambda b, h, t: (0, h)),
            pl.BlockSpec((1, M_DQK), lambda b, h, t: (0, M_HEADS + h)),
            pl.BlockSpec((1, SMALL_W), lambda b, h, t: (0, 0)),
            pl.BlockSpec((1, M_DV), lambda b, h, t: (0, h)),
            pl.BlockSpec((1, 1, M_DQK, M_DV), lambda b, h, t: (b, h, 0, 0)),
            pl.BlockSpec((1, 1, 1, M_DQK), lambda b, h, t: (b, h, 0, 0)),
            pl.BlockSpec((1, 1, 1, 128), lambda b, h, t: (b, h, 0, 0)),
        ],
        out_specs=[
            pl.BlockSpec((rb, M_DV), lambda b, h, t: (b * nt + t, h)),
            pl.BlockSpec((1, 1, M_DQK, M_DV), lambda b, h, t: (b, h, 0, 0)),
            pl.BlockSpec((1, 1, 1, M_DQK), lambda b, h, t: (b, h, 0, 0)),
            pl.BlockSpec((1, 1, 1, 128), lambda b, h, t: (b, h, 0, 0)),
        ],
        out_shape=[
            jax.ShapeDtypeStruct((nb * rows_per_seq, M_HEADS * M_DV), F32),
            jax.ShapeDtypeStruct((nb, M_HEADS, M_DQK, M_DV), F32),
            jax.ShapeDtypeStruct((nb, M_HEADS, 1, M_DQK), F32),
            jax.ShapeDtypeStruct((nb, M_HEADS, 1, 128), F32),
        ],
        scratch_shapes=[
            pltpu.VMEM((M_DQK, M_DV), F32),
            pltpu.VMEM((1, M_DQK), F32),
            pltpu.VMEM((1, 128), F32),
            pltpu.VMEM((8, M_DQK), F32),
            pltpu.VMEM((8, M_DQK), F32),
        ],
        compiler_params=_cparams(("parallel", "parallel", "arbitrary")),
        name="mlstm",
    )(z, z, z, z, zs, hist, hist, conv_w, conv_w, conv_b, conv_b, brow, norm, c0, n0, m0)


KV_W = N_KV * N_HD
HID_W = N_KV * CMP_HIDDEN


def _gelu(x):
    return 0.5 * x * (1.0 + lax.erf(x * (2.0 ** -0.5)))


def _rope_rows(x, cos, sin):
    lane = lax.broadcasted_iota(jnp.int32, x.shape, 1)
    half = ROPE_DIM // 2
    w = x.shape[1]
    partner = jnp.where(lane % N_HD < half, pltpu.roll(x, w - half, 1), pltpu.roll(x, half, 1))
    reps = w // cos.shape[1]
    return x * jnp.concatenate([cos] * reps, axis=1) + partner * jnp.concatenate([sin] * reps, axis=1)


def _store_heads(o_ref, row0, val):
    for kv in range(N_KV):
        o_ref[0, kv, row0:row0 + val.shape[0], :] = val[:, kv * N_HD:(kv + 1) * N_HD]


def _compress_blocks(lo_ref, hi_ref, nb, w1_ref, b1_ref, w2_ref):
    hid = jnp.zeros((nb, HID_W), F32)
    for j in range(CMP_BLOCK):
        xj = jnp.concatenate([lo_ref[pl.ds(j, nb, stride=CMP_BLOCK), :], hi_ref[pl.ds(j, nb, stride=CMP_BLOCK), :]], axis=1)
        hid = hid + _mm(xj, w1_ref[j])
    return _mm(_gelu(hid + b1_ref[...]), w2_ref[...])


def _compress_prompt_kernel(lo_ref, hi_ref, w1_ref, b1_ref, w2_ref, cos_ref, sin_ref, o_ref, *, nb):
    out = _compress_blocks(lo_ref, hi_ref, nb, w1_ref, b1_ref, w2_ref)
    _store_heads(o_ref, 0, _rope_rows(out, cos_ref[0:nb, :], sin_ref[0:nb, :]))


def _compress_prompt(z, B, T, col_off, wts, tabs):
    nb = T // CMP_BLOCK
    cos, sin = tabs
    w1, b1, w2 = wts
    kern = functools.partial(_compress_prompt_kernel, nb=nb)
    c2 = lambda b: (0, 0)
    return pl.pallas_call(
        kern,
        grid=(B,),
        in_specs=[pl.BlockSpec((T, 128), lambda b: (b, col_off // 128)),
                  pl.BlockSpec((T, 128), lambda b: (b, col_off // 128 + 1)),
                  pl.BlockSpec((CMP_BLOCK, KV_W, HID_W), lambda b: (0, 0, 0)),
                  pl.BlockSpec((1, HID_W), c2), pl.BlockSpec((HID_W, KV_W), c2),
                  pl.BlockSpec(cos.shape, c2), pl.BlockSpec(sin.shape, c2)],
        out_specs=pl.BlockSpec((1, N_KV, nb, N_HD), lambda b: (b, 0, 0, 0)),
        out_shape=jax.ShapeDtypeStruct((B, N_KV, nb, N_HD), F32),
        compiler_params=_cparams(("parallel",)),
        name="compress_prompt",
    )(z, z, w1, b1, w2, cos, sin)


def _page_copy(pt_ref, b, p, pool_ref, buf_ref, sem, rows_per_page):
    return pltpu.make_async_copy(pool_ref.at[pt_ref[b, p]],
                                 buf_ref.at[pl.ds(pl.multiple_of(p * rows_per_page, rows_per_page), rows_per_page)], sem)


def _page_half_copy(pt_ref, b, p, pool_ref, half, buf_ref, sem):
    return pltpu.make_async_copy(pool_ref.at[pt_ref[b, p], :, pl.ds(half * 128, 128)],
                                 buf_ref.at[pl.ds(pl.multiple_of(p * PAGE_SIZE, PAGE_SIZE), PAGE_SIZE)], sem)


def _compress_sample_kernel(pt_ref, pool_ref, new_ref, w1_ref, b1_ref, w2_ref, cos_ref, sin_ref, o_ref,
                            lo_buf, hi_buf, sem, *, n_pages, nb, n_new):
    b = pl.program_id(0)

    def start(p, c):
        _page_half_copy(pt_ref, b, p, pool_ref, 0, lo_buf, sem).start()
        _page_half_copy(pt_ref, b, p, pool_ref, 1, hi_buf, sem).start()
        return c

    def wait(p, c):
        _page_half_copy(pt_ref, b, p, pool_ref, 0, lo_buf, sem).wait()
        _page_half_copy(pt_ref, b, p, pool_ref, 1, hi_buf, sem).wait()
        return c

    lax.fori_loop(0, n_pages, start, 0)
    o_ref[...] = jnp.zeros(o_ref.shape, o_ref.dtype)
    xn = new_ref[...]
    hid = jnp.zeros((8, HID_W), F32)
    for jrow in range(n_new):
        hid = hid + _mm(jnp.broadcast_to(xn[jrow:jrow + 1, :], (8, KV_W)), w1_ref[jrow])
    row = lax.broadcasted_iota(jnp.int32, (8, HID_W), 0)
    hid = jnp.where(row == 0, hid, 0.0) + b1_ref[...]
    new = _mm(_gelu(hid), w2_ref[...])
    o_ref[0, nb:nb + 8, :] = _rope_rows(new, cos_ref[nb:nb + 8, :], sin_ref[nb:nb + 8, :])
    lax.fori_loop(0, n_pages, wait, 0)
    out = _compress_blocks(lo_buf, hi_buf, nb, w1_ref, b1_ref, w2_ref)
    o_ref[0, 0:nb, :] = _rope_rows(out, cos_ref[0:nb, :], sin_ref[0:nb, :])


def _compress_sample(page_table, pool, z, new_row_blk0, new_col_blk, wts, tabs, nbp, n_new):
    BS, n_pages = page_table.shape
    nb = n_pages * PAGE_SIZE // CMP_BLOCK
    cos, sin = tabs
    w1, b1, w2 = wts
    kern = functools.partial(_compress_sample_kernel, n_pages=n_pages, nb=nb, n_new=n_new)
    c2 = lambda b, pt: (0, 0)
    gs = pltpu.PrefetchScalarGridSpec(
        num_scalar_prefetch=1,
        grid=(BS,),
        in_specs=[pl.BlockSpec(memory_space=pl.ANY),
                  pl.BlockSpec((8, KV_W), lambda b, pt: (new_row_blk0 + b, new_col_blk)),
                  pl.BlockSpec((CMP_BLOCK, KV_W, HID_W), lambda b, pt: (0, 0, 0)),
                  pl.BlockSpec((1, HID_W), c2), pl.BlockSpec((HID_W, KV_W), c2),
                  pl.BlockSpec(cos.shape, c2), pl.BlockSpec(sin.shape, c2)],
        out_specs=pl.BlockSpec((1, nbp, KV_W), lambda b, pt: (b, 0, 0)),
        scratch_shapes=[pltpu.VMEM((n_pages * PAGE_SIZE, 128), F32), pltpu.VMEM((n_pages * PAGE_SIZE, 128), F32),
                        pltpu.SemaphoreType.DMA(())],
    )
    return pl.pallas_call(
        kern,
        grid_spec=gs,
        out_shape=jax.ShapeDtypeStruct((BS, nbp, KV_W), F32),
        compiler_params=_cparams(("arbitrary",)),
        name="compress_sample",
    )(page_table, pool, z, w1, b1, w2, cos, sin)


def _compress_weights(w1, b1, w2):
    eye = jnp.eye(N_KV, dtype=w1.dtype)
    w1big = jnp.einsum('jdh,kl->jkdlh', w1, eye).reshape(CMP_BLOCK, KV_W, HID_W)
    w2big = jnp.einsum('hd,kl->khld', w2, eye).reshape(HID_W, KV_W)
    return w1big.astype(MXU_DTYPE), jnp.tile(b1, N_KV).reshape(1, HID_W).astype(F32), w2big.astype(MXU_DTYPE)


def _block_tables(nbp, rope):
    cos, sin = _rope_tables(jnp.arange(nbp, dtype=jnp.int32) * CMP_BLOCK + CMP_BLOCK - 1)
    if not rope:
        cos, sin = jnp.ones_like(cos), jnp.zeros_like(sin)
    return cos, sin


SCALE = N_HD ** -0.5
INF = float('inf')


def _masked_softmax(s, mask):
    s = jnp.where(mask, s, NEG)
    m = jnp.max(s, axis=1, keepdims=True)
    e = jnp.where(mask, jnp.exp(s - m), 0.0)
    return e / jnp.maximum(jnp.sum(e, axis=1, keepdims=True), 1.0)


def _topk_mask(score, k):
    st = score.T
    rid = lax.broadcasted_iota(jnp.int32, st.shape, 0).astype(F32)

    def body(r, carry):
        s, sel = carry
        m = jnp.max(s, axis=0, keepdims=True)
        idx = jnp.min(jnp.where(s == m, rid, float(st.shape[0])), axis=0, keepdims=True)
        hit = rid == idx
        return jnp.where(hit, -INF, s), jnp.where(hit, 1.0, sel)

    _, sel = lax.fori_loop(0, k, body, (st, jnp.zeros(st.shape, F32)))
    return sel.T


def _stack_heads(q, tile):
    return jnp.concatenate([q[:, g * N_HD:(g + 1) * N_HD] for g in range(N_GROUP)], axis=0)


def _compressed_branch(qs, kc, vc, tq, nbc, wp, tile):
    nbp = kc.shape[0]
    blk = lax.broadcasted_iota(jnp.int32, (1, nbp), 1)
    s_c = _mm_nt(qs, kc)
    mask_c = (blk * CMP_BLOCK + CMP_BLOCK - 1 <= tq) & (blk < nbc)
    p_c = _masked_softmax(s_c, mask_c)
    o_cmp = _mm(p_c, vc)
    psum = p_c[0:tile]
    for g in range(1, N_GROUP):
        psum = psum + p_c[g * tile:(g + 1) * tile]
    ratio = SLC_BLOCK // CMP_BLOCK
    pair = jnp.where(lax.broadcasted_iota(jnp.int32, (nbp, wp), 0) // ratio
                     == lax.broadcasted_iota(jnp.int32, (nbp, wp), 1), 1.0, 0.0)
    return o_cmp, jnp.dot(psum, pair, precision=HIGHEST, preferred_element_type=F32)


def _select_blocks(imp, tq_t, n_sel, wp):
    sb = lax.broadcasted_iota(jnp.int32, (1, wp), 1)
    valid = (sb * SLC_BLOCK <= tq_t) & (sb < n_sel)
    forced = (sb == 0) | (sb == tq_t // SLC_BLOCK)
    score = jnp.where(valid, jnp.where(forced, INF, imp), -INF)
    return _topk_mask(score, min(SLC_TOPK, n_sel))


def _expand_sel(sel, kpos, wp):
    sbc = lax.broadcasted_iota(jnp.int32, (wp, 1), 0)
    e = jnp.where(sbc == kpos // SLC_BLOCK, 1.0, 0.0)
    return jnp.dot(sel.astype(jnp.bfloat16), e.astype(jnp.bfloat16), preferred_element_type=F32)


def _online_step(carry, s, mask, v):
    m, l, acc = carry
    s = jnp.where(mask, s, NEG)
    m_new = jnp.maximum(m, jnp.max(s, axis=1, keepdims=True))
    alpha = jnp.exp(m - m_new)
    p = jnp.where(mask, jnp.exp(s - m_new), 0.0)
    return m_new, alpha * l + jnp.sum(p, axis=1, keepdims=True), alpha * acc + _mm(p, v)


def _gate_col(gates, idx):
    col = lax.broadcasted_iota(jnp.int32, gates.shape, 1)
    return jnp.sum(jnp.where(col == idx, gates, 0.0), axis=1, keepdims=True)


def _combine(gates, head0, o_cmp, o_slc, o_win, tile):
    outs = []
    for g in range(N_GROUP):
        rows = slice(g * tile, (g + 1) * tile)
        c0 = SM_G + (head0 + g) * 3
        outs.append(_gate_col(gates, c0) * o_cmp[rows] + _gate_col(gates, c0 + 1) * o_slc[rows]
                    + _gate_col(gates, c0 + 2) * o_win[rows])
    return jnp.concatenate(outs, axis=1)


SLC_TILE = 256
DECODE_CHUNK = 2048
WIN_SPAN = WINDOW + Q_BLOCK


def _nsa_prompt_kernel(q_ref, zs_ref, kc_ref, vc_ref, ks_ref, vs_ref, kw_ref, vw_ref, o_ref, *, nbc, n_sel, wp):
    kv = pl.program_id(1)
    i = pl.program_id(2)
    tile = Q_BLOCK
    qs = _stack_heads(q_ref[...] * SCALE, tile).astype(MXU_DTYPE)
    trow = lax.broadcasted_iota(jnp.int32, (N_GROUP * tile, 1), 0) % tile
    tq = i * tile + trow
    tq_t = i * tile + lax.broadcasted_iota(jnp.int32, (tile, 1), 0)

    o_cmp, imp = _compressed_branch(qs, kc_ref[0, 0], vc_ref[0, 0], tq, nbc, wp, tile)
    sel = _select_blocks(imp, tq_t, n_sel, wp).astype(jnp.bfloat16)

    def one_tile(t, carry):
        m, acc = carry
        k0 = pl.multiple_of(t * SLC_TILE, SLC_TILE)
        kpos = k0 + lax.broadcasted_iota(jnp.int32, (1, SLC_TILE), 1)
        ok = (_expand_sel(sel, kpos, wp) > 0.5) & (kpos <= tq_t)
        bias = jnp.where(ok, 0.0, NEG)
        s = _mm_nt(qs, ks_ref[0, 0, pl.ds(k0, SLC_TILE), :]) + jnp.concatenate([bias] * N_GROUP, axis=0)
        m_new = jnp.maximum(m, jnp.max(s, axis=1, keepdims=True))
        p = jnp.exp(s - m_new)
        return m_new, jnp.exp(m - m_new) * acc + _mm(p, vs_ref[0, 0, pl.ds(k0, SLC_TILE), :])

    def slc_step(j, carry):
        return one_tile(2 * j + 1, one_tile(2 * j, carry))

    rows = N_GROUP * tile
    init = (jnp.full((rows, 1), NEG, F32), jnp.zeros((rows, 2 * N_HD), F32))
    n_pairs = (i * tile + tile - 1) // (2 * SLC_TILE) + 1
    _, acc = lax.fori_loop(0, n_pairs, slc_step, init)
    o_slc = acc[:, :N_HD] / jnp.maximum(acc[:, N_HD:N_HD + 1], 1.0)

    w0 = pl.multiple_of(jnp.maximum(i * tile - WINDOW, 0), tile)
    pw = w0 + lax.broadcasted_iota(jnp.int32, (1, WIN_SPAN), 1)
    dist = tq - pw
    s_w = _mm_nt(qs, kw_ref[0, 0, pl.ds(w0, WIN_SPAN), :])
    p_w = _masked_softmax(s_w, (dist >= 0) & (dist < WINDOW))
    o_win = _mm(p_w, vw_ref[0, 0, pl.ds(w0, WIN_SPAN), :])

    o_ref[...] = _combine(_sigmoid(zs_ref[...]), kv * N_GROUP, o_cmp, o_slc, o_win, tile)


def _nsa_prompt(z, zs, kc, vc, ks, vs, kw, vw, B, T, nbc, n_sel, wp):
    nq = T // Q_BLOCK
    nbp = kc.shape[2]
    kern = functools.partial(_nsa_prompt_kernel, nbc=nbc, n_sel=n_sel, wp=wp)
    seq = lambda b, kv, i: (b, kv, 0, 0)
    return pl.pallas_call(
        kern,
        grid=(B, N_KV, nq),
        in_specs=[pl.BlockSpec((Q_BLOCK, KV_W), lambda b, kv, i: (b * nq + i, OFF_NQ // KV_W + kv)),
                  pl.BlockSpec((Q_BLOCK, SMALL_W), lambda b, kv, i: (b * nq + i, 0)),
                  pl.BlockSpec((1, 1, nbp, N_HD), seq), pl.BlockSpec((1, 1, nbp, N_HD), seq),
                  pl.BlockSpec((1, 1, T, N_HD), seq), pl.BlockSpec((1, 1, T, 2 * N_HD), seq),
                  pl.BlockSpec((1, 1, T, N_HD), seq), pl.BlockSpec((1, 1, T, N_HD), seq)],
        out_specs=pl.BlockSpec((Q_BLOCK, KV_W), lambda b, kv, i: (b * nq + i, kv)),
        out_shape=jax.ShapeDtypeStruct((B * T, N_HEADS * N_HD), F32),
        compiler_params=_cparams(("parallel", "parallel", "arbitrary")),
        name="nsa_prompt",
    )(z, zs, kc, vc, ks, vs, kw, vw)


def _page_t_copy(pt_ref, b, p, pool_ref, buf_ref, sem):
    return pltpu.make_async_copy(pool_ref.at[pt_ref[b, p]],
                                 buf_ref.at[:, pl.ds(pl.multiple_of(p * PAGE_SIZE, PAGE_SIZE), PAGE_SIZE)], sem)


def _nsa_sample_kernel(pt_ref, q_ref, zs_ref, kc_ref, vc_ref, ksn_ref, vsn_ref, kwn_ref, vwn_ref, wk_ref, wv_ref,
                       e_ref, pk_ref, pv_ref, o_ref, kbuf, vbuf, sem_k, sem_v, *, past, n_new, nbc, n_sel, wp, chunk):
    b = pl.program_id(0)
    n_pages = past // PAGE_SIZE
    tile = SAMPLE_PAD
    rows = N_GROUP * tile

    def start(p, c):
        _page_t_copy(pt_ref, b, p, pk_ref, kbuf, sem_k).start()
        _page_t_copy(pt_ref, b, p, pv_ref, vbuf, sem_v).start()
        return c

    def wait(p, c):
        _page_t_copy(pt_ref, b, p, pk_ref, kbuf, sem_k).wait()
        _page_t_copy(pt_ref, b, p, pv_ref, vbuf, sem_v).wait()
        return c

    lax.fori_loop(0, n_pages, start, 0)

    q8 = q_ref[...] * SCALE
    lane = lax.broadcasted_iota(jnp.int32, (tile, KV_W), 1)
    qrows = []
    for kv in range(N_KV):
        sec = q8[:, kv * KV_W:(kv + 1) * KV_W]
        for g in range(N_GROUP):
            moved = sec if g == kv else pltpu.roll(sec, ((kv - g) % N_KV) * N_HD, 1)
            qrows.append(jnp.where(lane // N_HD == kv, moved, 0.0))
    qbd = jnp.concatenate(qrows, axis=0).astype(MXU_DTYPE)

    def own_head(full):
        return jnp.concatenate([full[kv * rows:(kv + 1) * rows, kv * N_HD:(kv + 1) * N_HD] for kv in range(N_KV)], axis=0)

    nrow = N_KV * rows
    tq = past + lax.broadcasted_iota(jnp.int32, (nrow, 1), 0) % tile
    newpos = past + lax.broadcasted_iota(jnp.int32, (1, tile), 1)
    new_ok = (newpos < past + n_new) & (newpos <= tq)

    kc, vc = kc_ref[0], vc_ref[0]
    nbp = kc.shape[0]
    blk = lax.broadcasted_iota(jnp.int32, (1, nbp), 1)
    p_c = _masked_softmax(_mm_nt(qbd, kc), (blk * CMP_BLOCK + CMP_BLOCK - 1 <= tq) & (blk < nbc))
    o_cmp = own_head(_mm(p_c, vc))
    psum = []
    for kv in range(N_KV):
        acc = p_c[kv * rows:kv * rows + tile]
        for g in range(1, N_GROUP):
            acc = acc + p_c[kv * rows + g * tile:kv * rows + (g + 1) * tile]
        psum.extend([acc] * N_GROUP)
    ratio = SLC_BLOCK // CMP_BLOCK
    pair = jnp.where(lax.broadcasted_iota(jnp.int32, (nbp, wp), 0) // ratio
                     == lax.broadcasted_iota(jnp.int32, (nbp, wp), 1), 1.0, 0.0)
    imp = jnp.dot(jnp.concatenate(psum, axis=0), pair, precision=HIGHEST, preferred_element_type=F32)
    sel_rows = _select_blocks(imp, tq, n_sel, wp).astype(jnp.bfloat16)

    wpos = past - WINDOW + lax.broadcasted_iota(jnp.int32, (1, WINDOW), 1)
    wdist = tq - wpos
    win_ok = (wdist >= 0) & (wdist < WINDOW) & (wpos >= 0)
    s1 = jnp.where(win_ok, _mm_nt(qbd, wk_ref[0]), NEG)
    s2 = jnp.where(new_ok, _mm_nt(qbd, kwn_ref[...]), NEG)
    m = jnp.maximum(jnp.max(s1, axis=1, keepdims=True), jnp.max(s2, axis=1, keepdims=True))
    e1 = jnp.where(win_ok, jnp.exp(s1 - m), 0.0)
    e2 = jnp.where(new_ok, jnp.exp(s2 - m), 0.0)
    den = jnp.maximum(jnp.sum(e1, axis=1, keepdims=True) + jnp.sum(e2, axis=1, keepdims=True), 1.0)
    o_win = own_head(_mm(e1, wv_ref[0]) + _mm(e2, vwn_ref[...])) / den

    lax.fori_loop(0, n_pages, wait, 0)

    carry = (jnp.full((nrow, 1), NEG, F32), jnp.zeros((nrow, 1), F32), jnp.zeros((nrow, KV_W), F32))
    for c in range(past // chunk):
        cs = slice(c * chunk, (c + 1) * chunk)
        s = jnp.dot(qbd, kbuf[:, cs].astype(MXU_DTYPE), preferred_element_type=F32)
        hit = jnp.dot(sel_rows, e_ref[:, cs], preferred_element_type=F32) > 0.5
        mm, l, acc = carry
        s = jnp.where(hit, s, NEG)
        m_new = jnp.maximum(mm, jnp.max(s, axis=1, keepdims=True))
        alpha = jnp.exp(mm - m_new)
        p = jnp.where(hit, jnp.exp(s - m_new), 0.0)
        carry = (m_new, alpha * l + jnp.sum(p, axis=1, keepdims=True), alpha * acc + _mm_nt(p, vbuf[:, cs]))
    hit_new = (_expand_sel(sel_rows, newpos, wp) > 0.5) & new_ok
    _, l, acc = _online_step(carry, _mm_nt(qbd, ksn_ref[...]), hit_new, vsn_ref[...])
    o_slc = own_head(acc) / jnp.maximum(l, 1.0)

    gates = _sigmoid(zs_ref[...])
    gcol = lambda c: jnp.concatenate([gates[:, SM_G + hh * 3 + c:SM_G + hh * 3 + c + 1] for hh in range(N_HEADS)], axis=0)
    comb = gcol(0) * o_cmp + gcol(1) * o_slc + gcol(2) * o_win
    o_ref[...] = jnp.concatenate([comb[hh * tile:(hh + 1) * tile] for hh in range(N_HEADS)], axis=1)


def _nsa_sample(page_table, z, zs, row_blk0, kc, vc, win_k, win_v, pool_k, pool_v, n_new, nbc, n_sel, wp):
    BS, n_pages = page_table.shape
    past = n_pages * PAGE_SIZE
    chunk = min(past, DECODE_CHUNK)
    kern = functools.partial(_nsa_sample_kernel, past=past, n_new=n_new, nbc=nbc, n_sel=n_sel, wp=wp, chunk=chunk)
    new = lambda off: pl.BlockSpec((SAMPLE_PAD, KV_W), lambda b, pt: (row_blk0 + b, off // KV_W))
    seq3 = lambda b, pt: (b, 0, 0)
    expand = (jnp.arange(wp, dtype=jnp.int32)[:, None] == jnp.arange(past, dtype=jnp.int32)[None, :] // SLC_BLOCK
              ).astype(jnp.bfloat16)
    gs = pltpu.PrefetchScalarGridSpec(
        num_scalar_prefetch=1,
        grid=(BS,),
        in_specs=[pl.BlockSpec((SAMPLE_PAD, N_HEADS * N_HD), lambda b, pt: (row_blk0 + b, OFF_NQ // (N_HEADS * N_HD))),
                  pl.BlockSpec((SAMPLE_PAD, SMALL_W), lambda b, pt: (row_blk0 + b, 0)),
                  pl.BlockSpec((1, kc.shape[1], KV_W), seq3), pl.BlockSpec((1, kc.shape[1], KV_W), seq3),
                  new(OFF_KS), new(OFF_VS), new(OFF_KW), new(OFF_VW),
                  pl.BlockSpec((1, WINDOW, KV_W), seq3), pl.BlockSpec((1, WINDOW, KV_W), seq3),
                  pl.BlockSpec((wp, past), lambda b, pt: (0, 0)),
                  pl.BlockSpec(memory_space=pl.ANY), pl.BlockSpec(memory_space=pl.ANY)],
        out_specs=pl.BlockSpec((SAMPLE_PAD, N_HEADS * N_HD), lambda b, pt: (b, 0)),
        scratch_shapes=[pltpu.VMEM((KV_W, past), F32), pltpu.VMEM((KV_W, past), F32),
                        pltpu.SemaphoreType.DMA(()), pltpu.SemaphoreType.DMA(())],
    )
    return pl.pallas_call(
        kern,
        grid_spec=gs,
        out_shape=jax.ShapeDtypeStruct((BS * SAMPLE_PAD, N_HEADS * N_HD), F32),
        compiler_params=_cparams(("arbitrary",)),
        name="nsa_sample",
    )(page_table, z, zs, kc, vc, z, z, z, z, win_k, win_v, expand, pool_k, pool_v)


MERGE_TILE = 256
P_QW = P_HEADS * P_DKEY
P_EXPERTS = P_NKEYS * P_NKEYS
P_HK = P_HEADS * P_TOPK


def _merge_kernel(x_ref, ha_ref, hb_ref, ga0_ref, ga1_ref, gb0_ref, gb1_ref, wo_ref, gain_ref, wq_ref,
                  x1_ref, xn_ref, qp_ref):
    ga = _sigmoid(jnp.concatenate([ga0_ref[...], ga1_ref[...]], axis=1))
    gb = _sigmoid(jnp.concatenate([gb0_ref[...], gb1_ref[...]], axis=1))
    x1 = x_ref[...] + _mm(ga * ha_ref[...] + gb * hb_ref[...], wo_ref[...])
    x1_ref[...] = x1
    xn = (x1 * lax.rsqrt(jnp.mean(x1 * x1, axis=-1, keepdims=True) + EPS) * gain_ref[...]).astype(MXU_DTYPE)
    xn_ref[...] = xn
    qp_ref[...] = jnp.dot(xn, wq_ref[...], preferred_element_type=F32)


def _merge(x_all, z, ha, hb, row0, w_out, gain, wq):
    n = ha.shape[0]
    tm = MERGE_TILE
    assert n % tm == 0 and row0 % tm == 0
    r0 = row0 // tm
    zc = lambda off: pl.BlockSpec((tm, COL_TILE), lambda i: (r0 + i, off // COL_TILE))
    full = lambda shp: pl.BlockSpec(shp, lambda i: (0, 0))
    loc = lambda w: pl.BlockSpec((tm, w), lambda i: (i, 0))
    return pl.pallas_call(
        _merge_kernel,
        grid=(n // tm,),
        in_specs=[pl.BlockSpec((tm, D_MODEL), lambda i: (r0 + i, 0)), loc(D_MODEL), loc(D_MODEL),
                  zc(OFF_GA), zc(OFF_GA + COL_TILE), zc(OFF_GB), zc(OFF_GB + COL_TILE),
                  full((D_MODEL, D_MODEL)), full((1, D_MODEL)), full((D_MODEL, P_QW))],
        out_specs=[loc(D_MODEL), loc(D_MODEL), loc(P_QW)],
        out_shape=[jax.ShapeDtypeStruct((n, D_MODEL), F32), jax.ShapeDtypeStruct((n, D_MODEL), MXU_DTYPE),
                   jax.ShapeDtypeStruct((n, P_QW), F32)],
        compiler_params=_cparams(("parallel",)),
        name="merge",
    )(x_all, ha, hb, z, z, z, z, w_out, gain, wq)


PEER_TILE = 256
PEER_EB = 1024
PEER_SLABS = PEER_EB // P_NKEYS


def _pop_topk_cols(s, k, emit):
    rows = s.shape[0]
    rid = lax.broadcasted_iota(jnp.int32, s.shape, 0).astype(F32)

    def body(r, s):
        m = jnp.max(s, axis=0, keepdims=True)
        idx = jnp.min(jnp.where(s == m, rid, float(rows)), axis=0, keepdims=True)
        hit = rid == idx
        emit(r, m, idx, hit)
        return jnp.where(hit, -INF, s)

    lax.fori_loop(0, k, body, s)


_CAND_GROUPS = [(0, 0), (0, 8)] + [(r0, 0) for r0 in range(1, P_TOPK)]


def _peer_select(qp_ref, sk_ref, a_scr, b_scr, g_scr, v_scr, i_scr, e_scr, bv_scr):
    tn = qp_ref.shape[0]
    sub8 = lax.broadcasted_iota(jnp.int32, (8, tn), 0)
    for h in range(P_HEADS):
        for c in range(2):
            hc = 2 * h + c
            s = _mm_nt(sk_ref[hc], qp_ref[:, hc * P_NKEYS:(hc + 1) * P_NKEYS])

            def emit(r, m, idx, hit, c=c):
                v_scr[pl.ds(c * P_TOPK + r, 1), :] = m
                i_scr[pl.ds(c * P_TOPK + r, 1), :] = idx

            _pop_topk_cols(s, P_TOPK, emit)
        cand, cexp = [], []
        for r0, r1 in _CAND_GROUPS:
            ok = (r0 + 1) * (r1 + sub8 + 1) <= P_TOPK
            cand.append(jnp.where(ok, v_scr[r0:r0 + 1, :] + v_scr[P_TOPK + r1:P_TOPK + r1 + 8, :], -INF))
            cexp.append(i_scr[r0:r0 + 1, :] * float(P_NKEYS) + i_scr[P_TOPK + r1:P_TOPK + r1 + 8, :])
        cand = jnp.concatenate(cand, axis=0)
        cexp = jnp.concatenate(cexp, axis=0)

        def emit2(r, m, idx, hit, h=h, cexp=cexp):
            e_scr[pl.ds(h * P_TOPK + r, 1), :] = jnp.sum(jnp.where(hit, cexp, 0.0), axis=0, keepdims=True)
            bv_scr[pl.ds(h * P_TOPK + r, 1), :] = m

        _pop_topk_cols(cand, P_TOPK, emit2)
    gates = []
    for h in range(P_HEADS):
        best = bv_scr[h * P_TOPK:(h + 1) * P_TOPK, :]
        ex = jnp.exp(best - best[0:1, :])
        gates.append(ex / jnp.sum(ex, axis=0, keepdims=True))
    e = e_scr[...]
    a = jnp.floor(e * (1.0 / P_NKEYS))
    a_scr[...] = a.T
    b_scr[...] = (e - a * float(P_NKEYS)).T
    g_scr[...] = jnp.concatenate(gates, axis=0).T


def _peer_kernel(x1_ref, xn_ref, qp_ref, sk_ref, u_ref, v_ref, gain_ref, y_ref,
                 a_scr, b_scr, g_scr, w_scr, acc_scr, v_scr, i_scr, e_scr, bv_scr):
    j = pl.program_id(1)
    tn = x1_ref.shape[0]

    @pl.when(j == 0)
    def _():
        _peer_select(qp_ref, sk_ref, a_scr, b_scr, g_scr, v_scr, i_scr, e_scr, bv_scr)
        sub = lax.broadcasted_iota(jnp.int32, (P_NKEYS, P_HK), 0).astype(F32)

        def per_group(n8, c):
            r0 = pl.multiple_of(n8 * 8, 8)
            a8, b8, g8 = a_scr[pl.ds(r0, 8), :], b_scr[pl.ds(r0, 8), :], g_scr[pl.ds(r0, 8), :]
            for t in range(8):
                at = jnp.where(sub == a8[t:t + 1], g8[t:t + 1], 0.0)
                bt = jnp.where(sub == b8[t:t + 1], 1.0, 0.0)
                w_scr[pl.ds(pl.multiple_of((r0 + t) * P_NKEYS, P_NKEYS), P_NKEYS), :] = _mm_nt(at, bt)
            return c

        lax.fori_loop(0, tn // 8, per_group, 0)
        acc_scr[...] = jnp.zeros(acc_scr.shape, F32)

    act = _gelu(_mm_nt(xn_ref[...], u_ref[...]))
    parts = []
    for r in range(PEER_SLABS):
        w = w_scr[pl.ds(j * PEER_SLABS + r, tn, stride=P_NKEYS), :]
        parts.append(w * act[:, r * P_NKEYS:(r + 1) * P_NKEYS])
    acc_scr[...] += _mm(jnp.concatenate(parts, axis=1), v_ref[...])

    @pl.when(j == pl.num_programs(1) - 1)
    def _():
        x2 = x1_ref[...] + acc_scr[...]
        y_ref[...] = x2 * lax.rsqrt(jnp.mean(x2 * x2, axis=-1, keepdims=True) + EPS) * gain_ref[...]


def _peer(x1, xn, qp, subkeys, u, v, gain):
    n = x1.shape[0]
    tn = PEER_TILE
    assert n % tn == 0
    tok = lambda w: pl.BlockSpec((tn, w), lambda i, j: (i, 0))
    return pl.pallas_call(
        _peer_kernel,
        grid=(n // tn, P_EXPERTS // PEER_EB),
        in_specs=[tok(D_MODEL), tok(D_MODEL), tok(P_QW),
                  pl.BlockSpec((2 * P_HEADS, P_NKEYS, P_DKEY // 2), lambda i, j: (0, 0, 0)),
                  pl.BlockSpec((PEER_EB, D_MODEL), lambda i, j: (j, 0)),
                  pl.BlockSpec((PEER_EB, D_MODEL), lambda i, j: (j, 0)),
                  pl.BlockSpec((1, D_MODEL), lambda i, j: (0, 0))],
        out_specs=tok(D_MODEL),
        out_shape=jax.ShapeDtypeStruct((n, D_MODEL), F32),
        scratch_shapes=[pltpu.VMEM((tn, P_HK), F32), pltpu.VMEM((tn, P_HK), F32), pltpu.VMEM((tn, P_HK), F32),
                        pltpu.VMEM((tn * P_NKEYS, P_NKEYS), F32), pltpu.VMEM((tn, D_MODEL), F32),
                        pltpu.VMEM((2 * P_TOPK, tn), F32), pltpu.VMEM((2 * P_TOPK, tn), F32),
                        pltpu.VMEM((P_HK, tn), F32), pltpu.VMEM((P_HK, tn), F32)],
        compiler_params=_cparams(("parallel", "arbitrary")),
        name="peer",
    )(x1, xn, qp, subkeys, u, v, gain)


def _split_w_in(w):
    widths = [2 * M_HEADS * M_DQK, M_HEADS * M_DV, M_HEADS, M_HEADS, M_HEADS * M_DV, N_HEADS * N_HD] + \
             [N_KV * N_HD] * 6 + [3 * N_HEADS, D_MODEL, D_MODEL]
    names = ['m_qk', 'm_v', 'm_i', 'm_f', 'm_o', 'n_q', 'n_kc', 'n_vc', 'n_ks', 'n_vs', 'n_kw', 'n_vw', 'n_g', 'g_a', 'g_b']
    out, off = {}, 0
    for nme, wd in zip(names, widths):
        out[nme] = w[:, off:off + wd]
        off += wd
    return out


def _stage_a(x_prompt, x_sample, past, norm_mix, w_in):
    B, T, D = x_prompt.shape
    BS, TS, _ = x_sample.shape
    xs_pad = jnp.pad(x_sample, ((0, 0), (0, SAMPLE_PAD - TS), (0, 0)))
    x_all = jnp.concatenate([x_prompt.reshape(B * T, D), xs_pad.reshape(BS * SAMPLE_PAD, D)], axis=0)
    rows = x_all.shape[0]
    rows_pad = -(-rows // ROW_TILE) * ROW_TILE
    x_all = jnp.pad(x_all, ((0, rows_pad - rows), (0, 0)))
    pos = jnp.concatenate([jnp.tile(jnp.arange(T, dtype=jnp.int32), B),
                           jnp.tile(past + jnp.arange(SAMPLE_PAD, dtype=jnp.int32), BS),
                           jnp.zeros((rows_pad - rows,), jnp.int32)])
    cos_t, sin_t = _rope_tables(pos)
    cols = _split_w_in(w_in)
    w_main = jnp.concatenate([cols[k] for k in ('m_qk', 'm_v', 'm_o', 'n_q', 'n_kc', 'n_vc', 'n_ks', 'n_vs',
                                                'n_kw', 'n_vw', 'g_a', 'g_b')], axis=1).astype(MXU_DTYPE)
    w_small = jnp.concatenate([cols['m_i'], cols['m_f'], cols['n_g'],
                               jnp.zeros((D, SMALL_W - 2 * M_HEADS - 3 * N_HEADS), w_in.dtype)], axis=1).astype(MXU_DTYPE)
    z, zs = _inproj(x_all, norm_mix.reshape(1, D), w_main, w_small, cos_t, sin_t)
    return x_all, z, zs


def kernel(x_prompt, x_sample, cache_cmp_k, cache_cmp_v, cache_slc_k, cache_slc_v, state_win_k, state_win_v, state_conv, state_C, state_n, state_m, page_table, norm_mix, w_in, m_conv_w, m_conv_b, m_bias_i, m_bias_f, m_norm, cmp_k_w1, cmp_k_b1, cmp_k_w2, cmp_v_w1, cmp_v_b1, cmp_v_w2, w_out, norm_ffn, peer_wq, peer_subkeys, peer_u, peer_v, norm_final):
    B, T, D = x_prompt.shape
    BS, TS, _ = x_sample.shape
    past = page_table.shape[1] * PAGE_SIZE
    assert TS <= SAMPLE_PAD and past >= WINDOW and T >= WIN_SPAN and T % ROW_TILE == 0
    n_rows_p, n_rows_s = B * T, BS * SAMPLE_PAD
    x_all, z, zs = _stage_a(x_prompt, x_sample, past, norm_mix[0], w_in[0])
    zeros = lambda *s: jnp.zeros(s, F32)

    brow = jnp.concatenate([m_bias_i[0], m_bias_f[0], zeros(SMALL_W - 2 * M_HEADS)]).reshape(1, -1)
    norm = m_norm[0].reshape(1, -1)
    cb = m_conv_b[0].reshape(1, -1)
    qk_w = 2 * M_HEADS * M_DQK
    ha_p, C_p, n_p, m_p = _mlstm(z, zs, 0, B, T, ROW_TILE, M_CHUNK, M_CHUNK, zeros(B, 8, qk_w), m_conv_w[0], cb, brow,
                                 norm, zeros(B, M_HEADS, M_DQK, M_DV), zeros(B, M_HEADS, 1, M_DQK),
                                 zeros(B, M_HEADS, 1, 128))
    hist = jnp.pad(state_conv[0], ((0, 0), (8 - (M_CONV - 1), 0), (0, 0)))
    m0 = jnp.broadcast_to(state_m[0][:, :, None, None], (BS, M_HEADS, 1, 128))
    ha_s, C_s, n_s, m_s = _mlstm(z, zs, n_rows_p, BS, SAMPLE_PAD, SAMPLE_PAD, SAMPLE_PAD, TS, hist, m_conv_w[0], cb,
                                 brow, norm, state_C[0], state_n[0][:, :, None, :], m0)

    sect = lambda off, r0, r1: z[r0:r1, off:off + KV_W]
    wk = _compress_weights(cmp_k_w1[0], cmp_k_b1[0], cmp_k_w2[0])
    wv = _compress_weights(cmp_v_w1[0], cmp_v_b1[0], cmp_v_w2[0])
    up128 = lambda n: -(-n // 128) * 128
    nbc_p = T // CMP_BLOCK
    n_sel_p = T // SLC_BLOCK
    kcmp_p, vcmp_p = sect(OFF_KC, 0, n_rows_p), sect(OFF_VC, 0, n_rows_p)
    kc_p = _compress_prompt(z, B, T, OFF_KC, wk, _block_tables(nbc_p, True))
    vc_p = _compress_prompt(z, B, T, OFF_VC, wv, _block_tables(nbc_p, False))
    prompt_sec = {off: sect(off, 0, n_rows_p) for off in (OFF_KS, OFF_VS, OFF_KW, OFF_VW)}
    heads_first = lambda a: a.reshape(B, T, N_KV, N_HD).transpose(0, 2, 1, 3).astype(MXU_DTYPE)
    ks_h, vs_h, kw_h, vw_h = [heads_first(prompt_sec[o]) for o in (OFF_KS, OFF_VS, OFF_KW, OFF_VW)]
    ones_col = (jnp.arange(N_HD) == 0).astype(MXU_DTYPE)
    vs_aug = jnp.concatenate([vs_h, jnp.broadcast_to(ones_col, vs_h.shape)], axis=-1)
    hb_p = _nsa_prompt(z, zs, kc_p, vc_p, ks_h, vs_aug, kw_h, vw_h, B, T, nbc_p, n_sel_p, up128(n_sel_p))

    nb_past = past // CMP_BLOCK
    nbc_s = nb_past + SLC_BLOCK // CMP_BLOCK
    n_sel_s = nbc_s * CMP_BLOCK // SLC_BLOCK
    nbp_s = up128(nb_past + 8)
    row_blk_s = n_rows_p // SAMPLE_PAD
    raw_pool = lambda pool: pool[0].reshape(-1, PAGE_SIZE, KV_W)
    kc_s = _compress_sample(page_table, raw_pool(cache_cmp_k), z, row_blk_s, OFF_KC // KV_W, wk,
                            _block_tables(nbp_s, True), nbp_s, TS)
    vc_s = _compress_sample(page_table, raw_pool(cache_cmp_v), z, row_blk_s, OFF_VC // KV_W, wv,
                            _block_tables(nbp_s, False), nbp_s, TS)
    n_win = state_win_k.shape[2]
    pool_t = lambda pool: pool[0].transpose(0, 2, 3, 1).reshape(-1, KV_W, PAGE_SIZE)
    hb_s = _nsa_sample(page_table, z, zs, row_blk_s, kc_s, vc_s, state_win_k[0].reshape(BS, n_win, KV_W),
                       state_win_v[0].reshape(BS, n_win, KV_W), pool_t(cache_slc_k), pool_t(cache_slc_v),
                       TS, nbc_s, n_sel_s, up128(n_sel_s))

    wo = w_out[0].astype(MXU_DTYPE)
    wq = peer_wq[0].astype(MXU_DTYPE)
    gain_ffn = norm_ffn[0].reshape(1, D)
    sk = peer_subkeys[0].reshape(2 * P_HEADS, P_NKEYS, P_DKEY // 2).astype(MXU_DTYPE)
    pu, pv = peer_u[0].astype(MXU_DTYPE), peer_v[0].astype(MXU_DTYPE)
    gain_out = norm_final.reshape(1, D)
    ys = []
    for ha, hb, row0 in ((ha_p, hb_p, 0), (ha_s, hb_s, n_rows_p)):
        x1, xn, qp = _merge(x_all, z, ha, hb, row0, wo, gain_ffn, wq)
        ys.append(_peer(x1, xn, qp, sk, pu, pv, gain_out))
    y_prompt = ys[0].reshape(B, T, D)
    y_sample = ys[1].reshape(BS, SAMPLE_PAD, D)[:, :TS]

    kvshape_p = (1, B, T, N_KV, N_HD)
    zsam = z[n_rows_p:n_rows_p + n_rows_s].reshape(BS, SAMPLE_PAD, Z_MAIN)[:, :TS]
    new_s = lambda off: zsam[:, :, off:off + KV_W].reshape(1, BS, TS, N_KV, N_HD)
    n_win_p = min(WINDOW, T)
    win_p = lambda off: prompt_sec[off].reshape(kvshape_p)[:, :, T - n_win_p:]
    win_s = lambda state, off: jnp.concatenate([state, new_s(off)], axis=2)[:, :, -n_win:]
    conv_p = z[:n_rows_p, OFF_MQ:OFF_MQ + qk_w].reshape(1, B, T, qk_w)[:, :, T - (M_CONV - 1):]
    conv_s = jnp.concatenate([state_conv, zsam[None, :, :, OFF_MQ:OFF_MQ + qk_w]], axis=2)[:, :, -(M_CONV - 1):]
    return (y_prompt, y_sample,
            kcmp_p.reshape(kvshape_p), new_s(OFF_KC), vcmp_p.reshape(kvshape_p), new_s(OFF_VC),
            prompt_sec[OFF_KS].reshape(kvshape_p), new_s(OFF_KS), prompt_sec[OFF_VS].reshape(kvshape_p), new_s(OFF_VS),
            win_p(OFF_KW), win_s(state_win_k, OFF_KW), win_p(OFF_VW), win_s(state_win_v, OFF_VW),
            conv_p, conv_s, C_p[None], C_s[None], n_p[None, :, :, 0], n_s[None, :, :, 0],
            m_p[None, :, :, 0, 0], m_s[None, :, :, 0, 0])
```

```python
import functools

import jax
import jax.numpy as jnp
from jax import lax
from jax.experimental import pallas as pl
from jax.experimental.pallas import tpu as pltpu

F32 = jnp.float32
MXU_DTYPE = jnp.bfloat16
HIGHEST = lax.Precision.HIGHEST
EPS = 1e-6
NEG = -1e30

D_MODEL = 1024
PAGE_SIZE = 128
M_HEADS, M_DQK, M_DV, M_CONV, M_CHUNK = 4, 128, 256, 4, 64
N_HEADS, N_KV, N_GROUP, N_HD = 16, 4, 4, 64
CMP_BLOCK, CMP_HIDDEN, SLC_BLOCK, SLC_TOPK, WINDOW, Q_BLOCK = 32, 128, 64, 16, 512, 128
ROPE_THETA, ROPE_DIM = 500000.0, 16
P_HEADS, P_NKEYS, P_DKEY, P_TOPK = 8, 128, 256, 16

OFF_MQ, OFF_MK, OFF_MV, OFF_MO, OFF_NQ = 0, 512, 1024, 2048, 3072
OFF_KC, OFF_VC, OFF_KS, OFF_VS, OFF_KW, OFF_VW = 4096, 4352, 4608, 4864, 5120, 5376
OFF_GA, OFF_GB, Z_MAIN = 5632, 6656, 7680
SMALL_W = 128
SM_I, SM_F, SM_G = 0, 4, 8
ROW_TILE = 1024
MLSTM_ROWS = 512
COL_TILE = 512
SAMPLE_PAD = 8
VMEM_LIMIT = 56 * 1024 * 1024


def _cparams(sem):
    return pltpu.CompilerParams(dimension_semantics=sem, vmem_limit_bytes=VMEM_LIMIT)


def _mm(a, b):
    return jnp.dot(a.astype(MXU_DTYPE), b.astype(MXU_DTYPE), preferred_element_type=F32)


def _mm_nt(a, b):
    return lax.dot_general(a.astype(MXU_DTYPE), b.astype(MXU_DTYPE), (((1,), (1,)), ((), ())),
                           preferred_element_type=F32)


def _mm_tn(a, b):
    return lax.dot_general(a.astype(MXU_DTYPE), b.astype(MXU_DTYPE), (((0,), (0,)), ((), ())),
                           preferred_element_type=F32)


def _log_sigmoid(x):
    return jnp.minimum(x, 0.0) - jnp.log(1.0 + jnp.exp(-jnp.abs(x)))


def _sigmoid(x):
    return 1.0 / (1.0 + jnp.exp(-x))


def _col_to_row(x_col):
    L = x_col.shape[0]
    ti = lax.broadcasted_iota(jnp.int32, (L, L), 0)
    si = lax.broadcasted_iota(jnp.int32, (L, L), 1)
    return jnp.sum(jnp.where(ti == si, x_col, 0.0), axis=0, keepdims=True)


def _inproj_kernel(x_ref, g_ref, w_ref, ws_ref, cos_ref, sin_ref, z_ref, zs_ref, xn_scr):
    j = pl.program_id(1)

    @pl.when(j == 0)
    def _():
        x = x_ref[...]
        r = x * lax.rsqrt(jnp.mean(x * x, axis=-1, keepdims=True) + EPS) * g_ref[...]
        xn = r.astype(MXU_DTYPE)
        xn_scr[...] = xn
        zs_ref[...] = jnp.dot(xn, ws_ref[...], preferred_element_type=F32)

    acc = jnp.dot(xn_scr[...], w_ref[...], preferred_element_type=F32)
    rope_full = (j == OFF_NQ // COL_TILE) | (j == OFF_NQ // COL_TILE + 1)
    rope_half = (j == OFF_KS // COL_TILE) | (j == OFF_KW // COL_TILE)
    is_rope = rope_full | rope_half

    @pl.when(is_rope)
    def _():
        lane = lax.broadcasted_iota(jnp.int32, acc.shape, 1)
        on = lane < jnp.where(rope_full, COL_TILE, COL_TILE // 2)
        reps = COL_TILE // cos_ref.shape[1]
        cos = jnp.where(on, jnp.concatenate([cos_ref[...]] * reps, axis=1), 1.0)
        sin = jnp.where(on, jnp.concatenate([sin_ref[...]] * reps, axis=1), 0.0)
        half = ROPE_DIM // 2
        partner = jnp.where(lane % N_HD < half, pltpu.roll(acc, COL_TILE - half, 1), pltpu.roll(acc, half, 1))
        z_ref[...] = acc * cos + partner * sin

    @pl.when(jnp.logical_not(is_rope))
    def _():
        z_ref[...] = acc


def _inproj(x_all, gain, w_main, w_small, cos_t, sin_t):
    rows = x_all.shape[0]
    grid = (rows // ROW_TILE, Z_MAIN // COL_TILE)
    return pl.pallas_call(
        _inproj_kernel,
        grid=grid,
        in_specs=[
            pl.BlockSpec((ROW_TILE, D_MODEL), lambda i, j: (i, 0)),
            pl.BlockSpec((1, D_MODEL), lambda i, j: (0, 0)),
            pl.BlockSpec((D_MODEL, COL_TILE), lambda i, j: (0, j)),
            pl.BlockSpec((D_MODEL, SMALL_W), lambda i, j: (0, 0)),
            pl.BlockSpec((ROW_TILE, 128), lambda i, j: (i, 0)),
            pl.BlockSpec((ROW_TILE, 128), lambda i, j: (i, 0)),
        ],
        out_specs=[
            pl.BlockSpec((ROW_TILE, COL_TILE), lambda i, j: (i, j)),
            pl.BlockSpec((ROW_TILE, SMALL_W), lambda i, j: (i, 0)),
        ],
        out_shape=[
            jax.ShapeDtypeStruct((rows, Z_MAIN), F32),
            jax.ShapeDtypeStruct((rows, SMALL_W), F32),
        ],
        scratch_shapes=[pltpu.VMEM((ROW_TILE, D_MODEL), MXU_DTYPE)],
        compiler_params=_cparams(("parallel", "arbitrary")),
        name="inproj",
    )(x_all, gain, w_main, w_small, cos_t, sin_t)


def _rope_tables(pos):
    half = ROPE_DIM // 2
    inv = ROPE_THETA ** (-jnp.arange(half, dtype=F32) / half)
    ang = pos.astype(F32)[:, None] * inv
    cos, sin = jnp.cos(ang), jnp.sin(ang)
    rows = pos.shape[0]
    pad1 = jnp.ones((rows, N_HD - ROPE_DIM), F32)
    pad0 = jnp.zeros((rows, N_HD - ROPE_DIM), F32)
    cos_h = jnp.concatenate([cos, cos, pad1], axis=1)
    sin_h = jnp.concatenate([-sin, sin, pad0], axis=1)
    return jnp.concatenate([cos_h, cos_h], axis=1), jnp.concatenate([sin_h, sin_h], axis=1)


def _mlstm_kernel(zq_ref, zk_ref, zv_ref, zo_ref, zs_ref, hq_ref, hk_ref, cwq_ref, cwk_ref, cbq_ref, cbk_ref,
                  brow_ref, norm_ref, c0_ref, n0_ref, m0_ref,
                  h_ref, c_ref, n_ref, m_ref, c_scr, n_scr, m_scr, hq_scr, hk_scr, *, chunk, nvalid):
    t = pl.program_id(2)
    h = pl.program_id(1)
    rb = zq_ref.shape[0]

    @pl.when(t == 0)
    def _():
        c_scr[...] = c0_ref[0, 0]
        n_scr[...] = n0_ref[0, 0]
        m_scr[...] = m0_ref[0, 0]
        hq_scr[...] = hq_ref[0]
        hk_scr[...] = hk_ref[0]

    def conv_silu(z_ref_, hist_scr, cw_ref, cb_ref):
        z = z_ref_[...]
        ext = jnp.concatenate([hist_scr[...], z], axis=0)
        acc = cb_ref[...]
        for jj in range(M_CONV):
            off = 8 - (M_CONV - 1) + jj
            acc = acc + ext[off:off + rb] * cw_ref[jj:jj + 1, :]
        hist_scr[...] = z[rb - 8:rb]
        return acc * _sigmoid(acc)

    q_all = conv_silu(zq_ref, hq_scr, cwq_ref, cbq_ref)
    k_all = conv_silu(zk_ref, hk_scr, cwk_ref, cbk_ref) * (M_DQK ** -0.5)

    zs = zs_ref[...] + brow_ref[...]
    col = lax.broadcasted_iota(jnp.int32, (rb, SMALL_W), 1)
    li_cols = jnp.sum(jnp.where(col == SM_I + h, zs, 0.0), axis=1, keepdims=True)
    lf_cols = _log_sigmoid(jnp.sum(jnp.where(col == SM_F + h, zs, 0.0), axis=1, keepdims=True))

    L = chunk
    ti = lax.broadcasted_iota(jnp.int32, (L, L), 0)
    si = lax.broadcasted_iota(jnp.int32, (L, L), 1)
    mask = (si <= ti) & (si < nvalid)
    tril = (si <= ti).astype(F32)
    tcol = lax.broadcasted_iota(jnp.int32, (L, 1), 0)

    C = c_scr[...]
    n = n_scr[...]
    m = m_scr[:, 0:1]
    gain = norm_ref[...]
    for c in range(rb // L):
        sl = slice(c * L, (c + 1) * L)
        q, k, v = q_all[sl], k_all[sl], zv_ref[sl, :]
        li_c = li_cols[sl]
        lf_c = jnp.where(tcol < nvalid, lf_cols[sl], 0.0)
        b_c = jnp.dot(tril, jnp.broadcast_to(lf_c, (L, 128)), precision=HIGHEST, preferred_element_type=F32)[:, 0:1]
        b_r = _col_to_row(b_c)
        li_r = _col_to_row(li_c)
        logw = jnp.where(mask, b_c - b_r + li_r, NEG)
        inter = b_c + m
        m_t = jnp.maximum(inter, jnp.max(logw, axis=1, keepdims=True))
        w = jnp.where(mask, jnp.exp(logw - m_t), 0.0) * _mm_nt(q, k)
        w_inter = jnp.exp(inter - m_t)
        num = _mm(w, v) + w_inter * _mm(q, C)
        den = jnp.sum(w, axis=1, keepdims=True) + w_inter * jnp.sum(q * n, axis=1, keepdims=True)
        hh = num / jnp.maximum(jnp.abs(den), jnp.exp(-m_t))
        b_end = b_c[L - 1:L]
        m_new = m_t[L - 1:L]
        w_end = jnp.where(tcol < nvalid, jnp.exp(b_end - b_c + li_c - m_new), 0.0)
        decay = jnp.exp(b_end + m - m_new)
        kw = w_end * k
        C = decay * C + _mm_tn(kw, v)
        n = decay * n + jnp.sum(kw, axis=0, keepdims=True)
        m = m_new
        hh = hh * lax.rsqrt(jnp.mean(hh * hh, axis=-1, keepdims=True) + EPS) * gain
        h_ref[sl, :] = _sigmoid(zo_ref[sl, :]) * hh

    c_scr[...] = C
    n_scr[...] = n
    m_scr[...] = jnp.broadcast_to(m, m_scr.shape)

    @pl.when(t == pl.num_programs(2) - 1)
    def _():
        c_ref[0, 0] = C
        n_ref[0, 0] = n
        m_ref[0, 0] = jnp.broadcast_to(m, m_scr.shape)


def _mlstm(z, zs, row0, nb, rows_per_seq, rb, chunk, nvalid, hist, conv_w, conv_b, brow, norm, c0, n0, m0):
    assert row0 % rb == 0 and rows_per_seq % rb == 0 and rb % chunk == 0
    nt = rows_per_seq // rb
    rblk = lambda b, t: row0 // rb + b * nt + t
    kern = functools.partial(_mlstm_kernel, chunk=chunk, nvalid=nvalid)
    return pl.pallas_call(
        kern,
        grid=(nb, M_HEADS, nt),
        in_specs=[
            pl.BlockSpec((rb, M_DQK), lambda b, h, t: (rblk(b, t), OFF_MQ // M_DQK + h)),
            pl.BlockSpec((rb, M_DQK), lambda b, h, t: (rblk(b, t), OFF_MK // M_DQK + h)),
            pl.BlockSpec((rb, M_DV), lambda b, h, t: (rblk(b, t), OFF_MV // M_DV + h)),
            pl.BlockSpec((rb, M_DV), lambda b, h, t: (rblk(b, t), OFF_MO // M_DV + h)),
            pl.BlockSpec((rb, SMALL_W), lambda b, h, t: (rblk(b, t), 0)),
            pl.BlockSpec((1, 8, M_DQK), lambda b, h, t: (b, 0, OFF_MQ // M_DQK + h)),
            pl.BlockSpec((1, 8, M_DQK), lambda b, h, t: (b, 0, OFF_MK // M_DQK + h)),
            pl.BlockSpec((M_CONV, M_DQK), lambda b, h, t: (0, h)),
            pl.BlockSpec((M_CONV, M_DQK), lambda b, h, t: (0, M_HEADS + h)),
            pl.BlockSpec((1, M_DQK), lambda b, h, t: (0, h)),
            pl.BlockSpec((1, M_DQK), lambda b, h, t: (0, M_HEADS + h)),
            pl.BlockSpec((1, SMALL_W), lambda b, h, t: (0, 0)),
            pl.BlockSpec((1, M_DV), lambda b, h, t: (0, h)),
            pl.BlockSpec((1, 1, M_DQK, M_DV), lambda b, h, t: (b, h, 0, 0)),
            pl.BlockSpec((1, 1, 1, M_DQK), lambda b, h, t: (b, h, 0, 0)),
            pl.BlockSpec((1, 1, 1, 128), lambda b, h, t: (b, h, 0, 0)),
        ],
        out_specs=[
            pl.BlockSpec((rb, M_DV), lambda b, h, t: (b * nt + t, h)),
            pl.BlockSpec((1, 1, M_DQK, M_DV), lambda b, h, t: (b, h, 0, 0)),
            pl.BlockSpec((1, 1, 1, M_DQK), lambda b, h, t: (b, h, 0, 0)),
            pl.BlockSpec((1, 1, 1, 128), lambda b, h, t: (b, h, 0, 0)),
        ],
        out_shape=[
            jax.ShapeDtypeStruct((nb * rows_per_seq, M_HEADS * M_DV), F32),
            jax.ShapeDtypeStruct((nb, M_HEADS, M_DQK, M_DV), F32),
            jax.ShapeDtypeStruct((nb, M_HEADS, 1, M_DQK), F32),
            jax.ShapeDtypeStruct((nb, M_HEADS, 1, 128), F32),
        ],
        scratch_shapes=[
            pltpu.VMEM((M_DQK, M_DV), F32),
            pltpu.VMEM((1, M_DQK), F32),
            pltpu.VMEM((1, 128), F32),
            pltpu.VMEM((8, M_DQK), F32),
            pltpu.VMEM((8, M_DQK), F32),
        ],
        compiler_params=_cparams(("parallel", "parallel", "arbitrary")),
        name="mlstm",
    )(z, z, z, z, zs, hist, hist, conv_w, conv_w, conv_b, conv_b, brow, norm, c0, n0, m0)


KV_W = N_KV * N_HD
HID_W = N_KV * CMP_HIDDEN
TRANSPOSE_UNROLL = 4
BLOCK_PITCH = CMP_BLOCK + 4


def _gelu(x):
    return 0.5 * x * (1.0 + lax.erf(x * (2.0 ** -0.5)))


def _rope_rows(x, cos, sin):
    lane = lax.broadcasted_iota(jnp.int32, x.shape, 1)
    half = ROPE_DIM // 2
    w = x.shape[1]
    partner = jnp.where(lane % N_HD < half, pltpu.roll(x, w - half, 1), pltpu.roll(x, half, 1))
    reps = w // cos.shape[1]
    return x * jnp.concatenate([cos] * reps, axis=1) + partner * jnp.concatenate([sin] * reps, axis=1)


def _store_heads(o_ref, row0, val):
    for kv in range(N_KV):
        o_ref[0, kv, row0:row0 + val.shape[0], :] = val[:, kv * N_HD:(kv + 1) * N_HD]


def _compress_blocks(lo_ref, hi_ref, nb, w1_ref, b1_ref, w2_ref, pitch=CMP_BLOCK):
    parts = []
    for j in range(CMP_BLOCK):
        parts.append(lo_ref[pl.ds(j, nb, stride=pitch), :].astype(MXU_DTYPE))
        parts.append(hi_ref[pl.ds(j, nb, stride=pitch), :].astype(MXU_DTYPE))
    hid = jnp.dot(jnp.concatenate(parts, axis=1), w1_ref[...], preferred_element_type=F32)
    return _mm(_gelu(hid + b1_ref[...]), w2_ref[...])


def _compress_prompt_kernel(lo_ref, hi_ref, w1_ref, b1_ref, w2_ref, cos_ref, sin_ref, o_ref, *, nb):
    out = _compress_blocks(lo_ref, hi_ref, nb, w1_ref, b1_ref, w2_ref)
    _store_heads(o_ref, 0, _rope_rows(out, cos_ref[0:nb, :], sin_ref[0:nb, :]))


def _compress_prompt(z, B, T, col_off, wts, tabs):
    nb = T // CMP_BLOCK
    cos, sin = tabs
    w1, b1, w2 = wts
    kern = functools.partial(_compress_prompt_kernel, nb=nb)
    c2 = lambda b: (0, 0)
    return pl.pallas_call(
        kern,
        grid=(B,),
        in_specs=[pl.BlockSpec((T, 128), lambda b: (b, col_off // 128)),
                  pl.BlockSpec((T, 128), lambda b: (b, col_off // 128 + 1)),
                  pl.BlockSpec((CMP_BLOCK * KV_W, HID_W), c2),
                  pl.BlockSpec((1, HID_W), c2), pl.BlockSpec((HID_W, KV_W), c2),
                  pl.BlockSpec(cos.shape, c2), pl.BlockSpec(sin.shape, c2)],
        out_specs=pl.BlockSpec((1, N_KV, nb, N_HD), lambda b: (b, 0, 0, 0)),
        out_shape=jax.ShapeDtypeStruct((B, N_KV, nb, N_HD), F32),
        compiler_params=_cparams(("parallel",)),
        name="compress_prompt",
    )(z, z, w1, b1, w2, cos, sin)


def _page_t_copy(pt_ref, b, p, pool_ref, buf_ref, sem):
    return pltpu.make_async_copy(pool_ref.at[pt_ref[b, p]],
                                 buf_ref.at[:, pl.ds(pl.multiple_of(p * PAGE_SIZE, PAGE_SIZE), PAGE_SIZE)], sem)


def _compress_sample_kernel(pt_ref, pool_ref, new_ref, w1_ref, b1_ref, w2_ref, cos_ref, sin_ref, o_ref,
                            stage, lo_buf, hi_buf, sem, *, n_pages, nb, n_new):
    b = pl.program_id(0)

    def start(p, c):
        _page_t_copy(pt_ref, b, p, pool_ref, stage, sem).start()
        return c

    def wait(p, c):
        _page_t_copy(pt_ref, b, p, pool_ref, stage, sem).wait()
        return c

    blocks_per_page = PAGE_SIZE // CMP_BLOCK

    def to_rows(p4, c):
        for u in range(TRANSPOSE_UNROLL):
            p = p4 * TRANSPOSE_UNROLL + u
            page = stage[:, pl.ds(pl.multiple_of(p * PAGE_SIZE, PAGE_SIZE), PAGE_SIZE)]
            base = pl.multiple_of(p * (blocks_per_page * BLOCK_PITCH), 8)
            for half, buf in ((0, lo_buf), (1, hi_buf)):
                rows = page[half * 128:(half + 1) * 128, :].T
                for bl in range(blocks_per_page):
                    buf[pl.ds(base + bl * BLOCK_PITCH, CMP_BLOCK), :] = rows[bl * CMP_BLOCK:(bl + 1) * CMP_BLOCK]
        return c

    lax.fori_loop(0, n_pages, start, 0)
    o_ref[...] = jnp.zeros(o_ref.shape, o_ref.dtype)
    xn = new_ref[...]
    hid = jnp.zeros((8, HID_W), F32)
    for jrow in range(n_new):
        hid = hid + _mm(jnp.broadcast_to(xn[jrow:jrow + 1, :], (8, KV_W)), w1_ref[jrow * KV_W:(jrow + 1) * KV_W, :])
    row = lax.broadcasted_iota(jnp.int32, (8, HID_W), 0)
    hid = jnp.where(row == 0, hid, 0.0) + b1_ref[...]
    new = _mm(_gelu(hid), w2_ref[...])
    o_ref[0, nb:nb + 8, :] = _rope_rows(new, cos_ref[nb:nb + 8, :], sin_ref[nb:nb + 8, :])
    lax.fori_loop(0, n_pages, wait, 0)
    lax.fori_loop(0, n_pages // TRANSPOSE_UNROLL, to_rows, 0)
    out = _compress_blocks(lo_buf, hi_buf, nb, w1_ref, b1_ref, w2_ref, pitch=BLOCK_PITCH)
    o_ref[0, 0:nb, :] = _rope_rows(out, cos_ref[0:nb, :], sin_ref[0:nb, :])


def _compress_sample(page_table, pool, z, new_row_blk0, new_col_blk, wts, tabs, nbp, n_new):
    BS, n_pages = page_table.shape
    nb = n_pages * PAGE_SIZE // CMP_BLOCK
    cos, sin = tabs
    w1, b1, w2 = wts
    kern = functools.partial(_compress_sample_kernel, n_pages=n_pages, nb=nb, n_new=n_new)
    c2 = lambda b, pt: (0, 0)
    gs = pltpu.PrefetchScalarGridSpec(
        num_scalar_prefetch=1,
        grid=(BS,),
        in_specs=[pl.BlockSpec(memory_space=pl.ANY),
                  pl.BlockSpec((8, KV_W), lambda b, pt: (new_row_blk0 + b, new_col_blk)),
                  pl.BlockSpec((CMP_BLOCK * KV_W, HID_W), c2),
                  pl.BlockSpec((1, HID_W), c2), pl.BlockSpec((HID_W, KV_W), c2),
                  pl.BlockSpec(cos.shape, c2), pl.BlockSpec(sin.shape, c2)],
        out_specs=pl.BlockSpec((1, nbp, KV_W), lambda b, pt: (b, 0, 0)),
        scratch_shapes=[pltpu.VMEM((KV_W, n_pages * PAGE_SIZE), F32),
                        pltpu.VMEM((nb * BLOCK_PITCH, 128), F32), pltpu.VMEM((nb * BLOCK_PITCH, 128), F32),
                        pltpu.SemaphoreType.DMA(())],
    )
    return pl.pallas_call(
        kern,
        grid_spec=gs,
        out_shape=jax.ShapeDtypeStruct((BS, nbp, KV_W), F32),
        compiler_params=_cparams(("arbitrary",)),
        name="compress_sample",
    )(page_table, pool, z, w1, b1, w2, cos, sin)


def _compress_weights(w1, b1, w2):
    eye = jnp.eye(N_KV, dtype=w1.dtype)
    w1big = jnp.einsum('jdh,kl->jkdlh', w1, eye).reshape(CMP_BLOCK * KV_W, HID_W)
    w2big = jnp.einsum('hd,kl->khld', w2, eye).reshape(HID_W, KV_W)
    return w1big.astype(MXU_DTYPE), jnp.tile(b1, N_KV).reshape(1, HID_W).astype(F32), w2big.astype(MXU_DTYPE)


def _block_tables(nbp, rope):
    cos, sin = _rope_tables(jnp.arange(nbp, dtype=jnp.int32) * CMP_BLOCK + CMP_BLOCK - 1)
    if not rope:
        cos, sin = jnp.ones_like(cos), jnp.zeros_like(sin)
    return cos, sin


SCALE = N_HD ** -0.5
INF = float('inf')


def _masked_softmax(s, mask):
    s = jnp.where(mask, s, NEG)
    m = jnp.max(s, axis=1, keepdims=True)
    e = jnp.where(mask, jnp.exp(s - m), 0.0)
    return e / jnp.maximum(jnp.sum(e, axis=1, keepdims=True), 1.0)


def _topk_mask(score, k):
    st = score.T
    rid = lax.broadcasted_iota(jnp.int32, st.shape, 0).astype(F32)

    def body(r, carry):
        s, sel = carry
        m = jnp.max(s, axis=0, keepdims=True)
        idx = jnp.min(jnp.where(s == m, rid, float(st.shape[0])), axis=0, keepdims=True)
        hit = rid == idx
        return jnp.where(hit, -INF, s), jnp.where(hit, 1.0, sel)

    _, sel = lax.fori_loop(0, k, body, (st, jnp.zeros(st.shape, F32)))
    return sel.T


def _stack_heads(q, tile):
    return jnp.concatenate([q[:, g * N_HD:(g + 1) * N_HD] for g in range(N_GROUP)], axis=0)


def _compressed_branch(qs, kc, vc, tq, nbc, wp, tile):
    nbp = kc.shape[0]
    blk = lax.broadcasted_iota(jnp.int32, (1, nbp), 1)
    s_c = _mm_nt(qs, kc)
    mask_c = (blk * CMP_BLOCK + CMP_BLOCK - 1 <= tq) & (blk < nbc)
    p_c = _masked_softmax(s_c, mask_c)
    o_cmp = _mm(p_c, vc)
    psum = p_c[0:tile]
    for g in range(1, N_GROUP):
        psum = psum + p_c[g * tile:(g + 1) * tile]
    ratio = SLC_BLOCK // CMP_BLOCK
    pair = jnp.where(lax.broadcasted_iota(jnp.int32, (nbp, wp), 0) // ratio
                     == lax.broadcasted_iota(jnp.int32, (nbp, wp), 1), 1.0, 0.0)
    return o_cmp, jnp.dot(psum, pair, precision=HIGHEST, preferred_element_type=F32)


def _select_blocks(imp, tq_t, n_sel, wp):
    sb = lax.broadcasted_iota(jnp.int32, (1, wp), 1)
    valid = (sb * SLC_BLOCK <= tq_t) & (sb < n_sel)
    forced = (sb == 0) | (sb == tq_t // SLC_BLOCK)
    score = jnp.where(valid, jnp.where(forced, INF, imp), -INF)
    return _topk_mask(score, min(SLC_TOPK, n_sel))


def _expand_sel(sel, kpos, wp):
    sbc = lax.broadcasted_iota(jnp.int32, (wp, 1), 0)
    e = jnp.where(sbc == kpos // SLC_BLOCK, 1.0, 0.0)
    return jnp.dot(sel.astype(jnp.bfloat16), e.astype(jnp.bfloat16), preferred_element_type=F32)


def _online_step(carry, s, mask, v):
    m, l, acc = carry
    s = jnp.where(mask, s, NEG)
    m_new = jnp.maximum(m, jnp.max(s, axis=1, keepdims=True))
    alpha = jnp.exp(m - m_new)
    p = jnp.where(mask, jnp.exp(s - m_new), 0.0)
    return m_new, alpha * l + jnp.sum(p, axis=1, keepdims=True), alpha * acc + _mm(p, v)


def _gate_col(gates, idx):
    col = lax.broadcasted_iota(jnp.int32, gates.shape, 1)
    return jnp.sum(jnp.where(col == idx, gates, 0.0), axis=1, keepdims=True)


def _combine(gates, head0, o_cmp, o_slc, o_win, tile):
    outs = []
    for g in range(N_GROUP):
        rows = slice(g * tile, (g + 1) * tile)
        c0 = SM_G + (head0 + g) * 3
        outs.append(_gate_col(gates, c0) * o_cmp[rows] + _gate_col(gates, c0 + 1) * o_slc[rows]
                    + _gate_col(gates, c0 + 2) * o_win[rows])
    return jnp.concatenate(outs, axis=1)


SLC_TILE = 256
DECODE_CHUNK = 2048
NSA_TILE = 256
WIN_SPAN = WINDOW + NSA_TILE


def _nsa_prompt_kernel(q_ref, zs_ref, kc_ref, vc_ref, ks_ref, vs_ref, kw_ref, vw_ref, o_ref, *, nbc, n_sel, wp):
    kv = pl.program_id(1)
    i = pl.program_id(2)
    tile = NSA_TILE
    qs = _stack_heads(q_ref[...] * SCALE, tile).astype(MXU_DTYPE)
    trow = lax.broadcasted_iota(jnp.int32, (N_GROUP * tile, 1), 0) % tile
    tq = i * tile + trow
    tq_t = i * tile + lax.broadcasted_iota(jnp.int32, (tile, 1), 0)

    o_cmp, imp = _compressed_branch(qs, kc_ref[0, 0], vc_ref[0, 0], tq, nbc, wp, tile)
    sel = _select_blocks(imp, tq_t, n_sel, wp).astype(jnp.bfloat16)

    def one_tile(t, carry):
        m, acc = carry
        k0 = pl.multiple_of(t * SLC_TILE, SLC_TILE)
        kpos = k0 + lax.broadcasted_iota(jnp.int32, (1, SLC_TILE), 1)
        ok = (_expand_sel(sel, kpos, wp) > 0.5) & (kpos <= tq_t)
        bias = jnp.where(ok, 0.0, NEG)
        s = _mm_nt(qs, ks_ref[0, 0, pl.ds(k0, SLC_TILE), :]) + jnp.concatenate([bias] * N_GROUP, axis=0)
        m_new = jnp.maximum(m, jnp.max(s, axis=1, keepdims=True))
        p = jnp.exp(s - m_new)
        return m_new, jnp.exp(m - m_new) * acc + _mm(p, vs_ref[0, 0, pl.ds(k0, SLC_TILE), :])

    def slc_step(j, carry):
        return one_tile(2 * j + 1, one_tile(2 * j, carry))

    rows = N_GROUP * tile
    init = (jnp.full((rows, 1), NEG, F32), jnp.zeros((rows, 2 * N_HD), F32))
    n_pairs = (i * tile + tile - 1) // (2 * SLC_TILE) + 1
    _, acc = lax.fori_loop(0, n_pairs, slc_step, init)
    o_slc = acc[:, :N_HD] / jnp.maximum(acc[:, N_HD:N_HD + 1], 1.0)

    w0 = pl.multiple_of(jnp.maximum(i * tile - WINDOW, 0), tile)
    pw = w0 + lax.broadcasted_iota(jnp.int32, (1, WIN_SPAN), 1)
    dist = tq - pw
    s_w = _mm_nt(qs, kw_ref[0, 0, pl.ds(w0, WIN_SPAN), :])
    p_w = _masked_softmax(s_w, (dist >= 0) & (dist < WINDOW))
    o_win = _mm(p_w, vw_ref[0, 0, pl.ds(w0, WIN_SPAN), :])

    o_ref[...] = _combine(_sigmoid(zs_ref[...]), kv * N_GROUP, o_cmp, o_slc, o_win, tile)


def _nsa_prompt(z, zs, kc, vc, ks, vs, kw, vw, B, T, nbc, n_sel, wp):
    nq = T // NSA_TILE
    nbp = kc.shape[2]
    kern = functools.partial(_nsa_prompt_kernel, nbc=nbc, n_sel=n_sel, wp=wp)
    seq = lambda b, kv, i: (b, kv, 0, 0)
    return pl.pallas_call(
        kern,
        grid=(B, N_KV, nq),
        in_specs=[pl.BlockSpec((NSA_TILE, KV_W), lambda b, kv, i: (b * nq + i, OFF_NQ // KV_W + kv)),
                  pl.BlockSpec((NSA_TILE, SMALL_W), lambda b, kv, i: (b * nq + i, 0)),
                  pl.BlockSpec((1, 1, nbp, N_HD), seq), pl.BlockSpec((1, 1, nbp, N_HD), seq),
                  pl.BlockSpec((1, 1, T, N_HD), seq), pl.BlockSpec((1, 1, T, 2 * N_HD), seq),
                  pl.BlockSpec((1, 1, T, N_HD), seq), pl.BlockSpec((1, 1, T, N_HD), seq)],
        out_specs=pl.BlockSpec((NSA_TILE, KV_W), lambda b, kv, i: (b * nq + i, kv)),
        out_shape=jax.ShapeDtypeStruct((B * T, N_HEADS * N_HD), F32),
        compiler_params=_cparams(("parallel", "parallel", "arbitrary")),
        name="nsa_prompt",
    )(z, zs, kc, vc, ks, vs, kw, vw)


def _nsa_sample_kernel(pt_ref, q_ref, zs_ref, kc_ref, vc_ref, ksn_ref, vsn_ref, kwn_ref, vwn_ref, wk_ref, wv_ref,
                       e_ref, pk_ref, pv_ref, o_ref, kbuf, vbuf, sem_k, sem_v, *, past, n_new, nbc, n_sel, wp, chunk):
    b = pl.program_id(0)
    n_pages = past // PAGE_SIZE
    tile = SAMPLE_PAD
    rows = N_GROUP * tile

    def start(p, c):
        _page_t_copy(pt_ref, b, p, pk_ref, kbuf, sem_k).start()
        _page_t_copy(pt_ref, b, p, pv_ref, vbuf, sem_v).start()
        return c

    def wait(p, c):
        _page_t_copy(pt_ref, b, p, pk_ref, kbuf, sem_k).wait()
        _page_t_copy(pt_ref, b, p, pv_ref, vbuf, sem_v).wait()
        return c

    lax.fori_loop(0, n_pages, start, 0)

    q8 = q_ref[...] * SCALE
    lane = lax.broadcasted_iota(jnp.int32, (tile, KV_W), 1)
    qrows = []
    for kv in range(N_KV):
        sec = q8[:, kv * KV_W:(kv + 1) * KV_W]
        for g in range(N_GROUP):
            moved = sec if g == kv else pltpu.roll(sec, ((kv - g) % N_KV) * N_HD, 1)
            qrows.append(jnp.where(lane // N_HD == kv, moved, 0.0))
    qbd = jnp.concatenate(qrows, axis=0).astype(MXU_DTYPE)

    def own_head(full):
        return jnp.concatenate([full[kv * rows:(kv + 1) * rows, kv * N_HD:(kv + 1) * N_HD] for kv in range(N_KV)], axis=0)

    nrow = N_KV * rows
    tq = past + lax.broadcasted_iota(jnp.int32, (nrow, 1), 0) % tile
    newpos = past + lax.broadcasted_iota(jnp.int32, (1, tile), 1)
    new_ok = (newpos < past + n_new) & (newpos <= tq)

    kc, vc = kc_ref[0], vc_ref[0]
    nbp = kc.shape[0]
    blk = lax.broadcasted_iota(jnp.int32, (1, nbp), 1)
    p_c = _masked_softmax(_mm_nt(qbd, kc), (blk * CMP_BLOCK + CMP_BLOCK - 1 <= tq) & (blk < nbc))
    o_cmp = own_head(_mm(p_c, vc))
    psum = []
    for kv in range(N_KV):
        acc = p_c[kv * rows:kv * rows + tile]
        for g in range(1, N_GROUP):
            acc = acc + p_c[kv * rows + g * tile:kv * rows + (g + 1) * tile]
        psum.extend([acc] * N_GROUP)
    ratio = SLC_BLOCK // CMP_BLOCK
    pair = jnp.where(lax.broadcasted_iota(jnp.int32, (nbp, wp), 0) // ratio
                     == lax.broadcasted_iota(jnp.int32, (nbp, wp), 1), 1.0, 0.0)
    imp = jnp.dot(jnp.concatenate(psum, axis=0), pair, precision=HIGHEST, preferred_element_type=F32)
    sel_rows = _select_blocks(imp, tq, n_sel, wp).astype(jnp.bfloat16)

    wpos = past - WINDOW + lax.broadcasted_iota(jnp.int32, (1, WINDOW), 1)
    wdist = tq - wpos
    win_ok = (wdist >= 0) & (wdist < WINDOW) & (wpos >= 0)
    s1 = jnp.where(win_ok, _mm_nt(qbd, wk_ref[0]), NEG)
    s2 = jnp.where(new_ok, _mm_nt(qbd, kwn_ref[...]), NEG)
    m = jnp.maximum(jnp.max(s1, axis=1, keepdims=True), jnp.max(s2, axis=1, keepdims=True))
    e1 = jnp.where(win_ok, jnp.exp(s1 - m), 0.0)
    e2 = jnp.where(new_ok, jnp.exp(s2 - m), 0.0)
    den = jnp.maximum(jnp.sum(e1, axis=1, keepdims=True) + jnp.sum(e2, axis=1, keepdims=True), 1.0)
    o_win = own_head(_mm(e1, wv_ref[0]) + _mm(e2, vwn_ref[...])) / den

    lax.fori_loop(0, n_pages, wait, 0)

    carry = (jnp.full((nrow, 1), NEG, F32), jnp.zeros((nrow, 1), F32), jnp.zeros((nrow, KV_W), F32))
    for c in range(past // chunk):
        cs = slice(c * chunk, (c + 1) * chunk)
        s = jnp.dot(qbd, kbuf[:, cs].astype(MXU_DTYPE), preferred_element_type=F32)
        hit = jnp.dot(sel_rows, e_ref[:, cs], preferred_element_type=F32) > 0.5
        mm, l, acc = carry
        s = jnp.where(hit, s, NEG)
        m_new = jnp.maximum(mm, jnp.max(s, axis=1, keepdims=True))
        alpha = jnp.exp(mm - m_new)
        p = jnp.where(hit, jnp.exp(s - m_new), 0.0)
        carry = (m_new, alpha * l + jnp.sum(p, axis=1, keepdims=True), alpha * acc + _mm_nt(p, vbuf[:, cs]))
    hit_new = (_expand_sel(sel_rows, newpos, wp) > 0.5) & new_ok
    _, l, acc = _online_step(carry, _mm_nt(qbd, ksn_ref[...]), hit_new, vsn_ref[...])
    o_slc = own_head(acc) / jnp.maximum(l, 1.0)

    gates = _sigmoid(zs_ref[...])
    gcol = lambda c: jnp.concatenate([gates[:, SM_G + hh * 3 + c:SM_G + hh * 3 + c + 1] for hh in range(N_HEADS)], axis=0)
    comb = gcol(0) * o_cmp + gcol(1) * o_slc + gcol(2) * o_win
    o_ref[...] = jnp.concatenate([comb[hh * tile:(hh + 1) * tile] for hh in range(N_HEADS)], axis=1)


def _nsa_sample(page_table, z, zs, row_blk0, kc, vc, win_k, win_v, pool_k, pool_v, n_new, nbc, n_sel, wp):
    BS, n_pages = page_table.shape
    past = n_pages * PAGE_SIZE
    chunk = min(past, DECODE_CHUNK)
    kern = functools.partial(_nsa_sample_kernel, past=past, n_new=n_new, nbc=nbc, n_sel=n_sel, wp=wp, chunk=chunk)
    new = lambda off: pl.BlockSpec((SAMPLE_PAD, KV_W), lambda b, pt: (row_blk0 + b, off // KV_W))
    seq3 = lambda b, pt: (b, 0, 0)
    expand = (jnp.arange(wp, dtype=jnp.int32)[:, None] == jnp.arange(past, dtype=jnp.int32)[None, :] // SLC_BLOCK
              ).astype(jnp.bfloat16)
    gs = pltpu.PrefetchScalarGridSpec(
        num_scalar_prefetch=1,
        grid=(BS,),
        in_specs=[pl.BlockSpec((SAMPLE_PAD, N_HEADS * N_HD), lambda b, pt: (row_blk0 + b, OFF_NQ // (N_HEADS * N_HD))),
                  pl.BlockSpec((SAMPLE_PAD, SMALL_W), lambda b, pt: (row_blk0 + b, 0)),
                  pl.BlockSpec((1, kc.shape[1], KV_W), seq3), pl.BlockSpec((1, kc.shape[1], KV_W), seq3),
                  new(OFF_KS), new(OFF_VS), new(OFF_KW), new(OFF_VW),
                  pl.BlockSpec((1, WINDOW, KV_W), seq3), pl.BlockSpec((1, WINDOW, KV_W), seq3),
                  pl.BlockSpec((wp, past), lambda b, pt: (0, 0)),
                  pl.BlockSpec(memory_space=pl.ANY), pl.BlockSpec(memory_space=pl.ANY)],
        out_specs=pl.BlockSpec((SAMPLE_PAD, N_HEADS * N_HD), lambda b, pt: (b, 0)),
        scratch_shapes=[pltpu.VMEM((KV_W, past), F32), pltpu.VMEM((KV_W, past), F32),
                        pltpu.SemaphoreType.DMA(()), pltpu.SemaphoreType.DMA(())],
    )
    return pl.pallas_call(
        kern,
        grid_spec=gs,
        out_shape=jax.ShapeDtypeStruct((BS * SAMPLE_PAD, N_HEADS * N_HD), F32),
        compiler_params=_cparams(("arbitrary",)),
        name="nsa_sample",
    )(page_table, z, zs, kc, vc, z, z, z, z, win_k, win_v, expand, pool_k, pool_v)


MERGE_TILE = 256
P_QW = P_HEADS * P_DKEY
P_EXPERTS = P_NKEYS * P_NKEYS
P_HK = P_HEADS * P_TOPK


def _merge_kernel(x_ref, ha_ref, hb_ref, ga0_ref, ga1_ref, gb0_ref, gb1_ref, wo_ref, gain_ref, wq_ref,
                  x1_ref, xn_ref, qp_ref):
    ga = _sigmoid(jnp.concatenate([ga0_ref[...], ga1_ref[...]], axis=1))
    gb = _sigmoid(jnp.concatenate([gb0_ref[...], gb1_ref[...]], axis=1))
    x1 = x_ref[...] + _mm(ga * ha_ref[...] + gb * hb_ref[...], wo_ref[...])
    x1_ref[...] = x1
    xn = (x1 * lax.rsqrt(jnp.mean(x1 * x1, axis=-1, keepdims=True) + EPS) * gain_ref[...]).astype(MXU_DTYPE)
    xn_ref[...] = xn
    qp_ref[...] = jnp.dot(xn, wq_ref[...], preferred_element_type=F32)


def _merge(x_all, z, ha, hb, row0, w_out, gain, wq):
    n = ha.shape[0]
    tm = MERGE_TILE
    assert n % tm == 0 and row0 % tm == 0
    r0 = row0 // tm
    zc = lambda off: pl.BlockSpec((tm, COL_TILE), lambda i: (r0 + i, off // COL_TILE))
    full = lambda shp: pl.BlockSpec(shp, lambda i: (0, 0))
    loc = lambda w: pl.BlockSpec((tm, w), lambda i: (i, 0))
    return pl.pallas_call(
        _merge_kernel,
        grid=(n // tm,),
        in_specs=[pl.BlockSpec((tm, D_MODEL), lambda i: (r0 + i, 0)), loc(D_MODEL), loc(D_MODEL),
                  zc(OFF_GA), zc(OFF_GA + COL_TILE), zc(OFF_GB), zc(OFF_GB + COL_TILE),
                  full((D_MODEL, D_MODEL)), full((1, D_MODEL)), full((D_MODEL, P_QW))],
        out_specs=[loc(D_MODEL), loc(D_MODEL), loc(P_QW)],
        out_shape=[jax.ShapeDtypeStruct((n, D_MODEL), F32), jax.ShapeDtypeStruct((n, D_MODEL), MXU_DTYPE),
                   jax.ShapeDtypeStruct((n, P_QW), F32)],
        compiler_params=_cparams(("parallel",)),
        name="merge",
    )(x_all, ha, hb, z, z, z, z, w_out, gain, wq)


PEER_TILE = 256
PEER_EB = 1024
PEER_SLABS = PEER_EB // P_NKEYS
W_PITCH = P_NKEYS + 4


def _pop_topk_cols(s, k, emit):
    rows = s.shape[0]
    rid = lax.broadcasted_iota(jnp.int32, s.shape, 0).astype(F32)

    def body(r, s):
        m = jnp.max(s, axis=0, keepdims=True)
        idx = jnp.min(jnp.where(s == m, rid, float(rows)), axis=0, keepdims=True)
        hit = rid == idx
        emit(r, m, idx, hit)
        return jnp.where(hit, -INF, s)

    lax.fori_loop(0, k, body, s)


_CAND_GROUPS = [(0, 0), (0, 8)] + [(r0, 0) for r0 in range(1, P_TOPK)]


def _peer_select(qp_ref, sk_ref, a_scr, b_scr, g_scr, v_scr, i_scr, e_scr, bv_scr):
    tn = qp_ref.shape[0]
    sub8 = lax.broadcasted_iota(jnp.int32, (8, tn), 0)
    for h in range(P_HEADS):
        for c in range(2):
            hc = 2 * h + c
            s = _mm_nt(sk_ref[hc], qp_ref[:, hc * P_NKEYS:(hc + 1) * P_NKEYS])

            def emit(r, m, idx, hit, c=c):
                v_scr[pl.ds(c * P_TOPK + r, 1), :] = m
                i_scr[pl.ds(c * P_TOPK + r, 1), :] = idx

            _pop_topk_cols(s, P_TOPK, emit)
        cand, cexp = [], []
        for r0, r1 in _CAND_GROUPS:
            ok = (r0 + 1) * (r1 + sub8 + 1) <= P_TOPK
            cand.append(jnp.where(ok, v_scr[r0:r0 + 1, :] + v_scr[P_TOPK + r1:P_TOPK + r1 + 8, :], -INF))
            cexp.append(i_scr[r0:r0 + 1, :] * float(P_NKEYS) + i_scr[P_TOPK + r1:P_TOPK + r1 + 8, :])
        cand = jnp.concatenate(cand, axis=0)
        cexp = jnp.concatenate(cexp, axis=0)

        def emit2(r, m, idx, hit, h=h, cexp=cexp):
            e_scr[pl.ds(h * P_TOPK + r, 1), :] = jnp.sum(jnp.where(hit, cexp, 0.0), axis=0, keepdims=True)
            bv_scr[pl.ds(h * P_TOPK + r, 1), :] = m

        _pop_topk_cols(cand, P_TOPK, emit2)
    gates = []
    for h in range(P_HEADS):
        best = bv_scr[h * P_TOPK:(h + 1) * P_TOPK, :]
        ex = jnp.exp(best - best[0:1, :])
        gates.append(ex / jnp.sum(ex, axis=0, keepdims=True))
    e = e_scr[...]
    a = jnp.floor(e * (1.0 / P_NKEYS))
    a_scr[...] = a.T
    b_scr[...] = (e - a * float(P_NKEYS)).T
    g_scr[...] = jnp.concatenate(gates, axis=0).T


def _peer_kernel(x1_ref, xn_ref, qp_ref, sk_ref, u_ref, v_ref, gain_ref, y_ref,
                 a_scr, b_scr, g_scr, w_scr, acc_scr, v_scr, i_scr, e_scr, bv_scr):
    j = pl.program_id(1)
    tn = x1_ref.shape[0]

    @pl.when(j == 0)
    def _():
        _peer_select(qp_ref, sk_ref, a_scr, b_scr, g_scr, v_scr, i_scr, e_scr, bv_scr)
        sub = lax.broadcasted_iota(jnp.int32, (P_NKEYS, P_HK), 0).astype(F32)

        def per_group(n8, c):
            r0 = pl.multiple_of(n8 * 8, 8)
            a8, b8, g8 = a_scr[pl.ds(r0, 8), :], b_scr[pl.ds(r0, 8), :], g_scr[pl.ds(r0, 8), :]
            for t in range(8):
                at = jnp.where(sub == a8[t:t + 1], g8[t:t + 1], 0.0)
                bt = jnp.where(sub == b8[t:t + 1], 1.0, 0.0)
                w_scr[pl.ds(pl.multiple_of(r0 * W_PITCH, 8) + t * W_PITCH, P_NKEYS), :] = _mm_nt(at, bt)
            return c

        lax.fori_loop(0, tn // 8, per_group, 0)
        acc_scr[...] = jnp.zeros(acc_scr.shape, F32)

    act = _gelu(_mm_nt(xn_ref[...], u_ref[...]))
    parts = []
    for r in range(PEER_SLABS):
        w = w_scr[pl.ds(j * PEER_SLABS + r, tn, stride=W_PITCH), :]
        parts.append(w * act[:, r * P_NKEYS:(r + 1) * P_NKEYS])
    acc_scr[...] += _mm(jnp.concatenate(parts, axis=1), v_ref[...])

    @pl.when(j == pl.num_programs(1) - 1)
    def _():
        x2 = x1_ref[...] + acc_scr[...]
        y_ref[...] = x2 * lax.rsqrt(jnp.mean(x2 * x2, axis=-1, keepdims=True) + EPS) * gain_ref[...]


def _peer(x1, xn, qp, subkeys, u, v, gain):
    n = x1.shape[0]
    tn = PEER_TILE
    assert n % tn == 0
    tok = lambda w: pl.BlockSpec((tn, w), lambda i, j: (i, 0))
    return pl.pallas_call(
        _peer_kernel,
        grid=(n // tn, P_EXPERTS // PEER_EB),
        in_specs=[tok(D_MODEL), tok(D_MODEL), tok(P_QW),
                  pl.BlockSpec((2 * P_HEADS, P_NKEYS, P_DKEY // 2), lambda i, j: (0, 0, 0)),
                  pl.BlockSpec((PEER_EB, D_MODEL), lambda i, j: (j, 0)),
                  pl.BlockSpec((PEER_EB, D_MODEL), lambda i, j: (j, 0)),
                  pl.BlockSpec((1, D_MODEL), lambda i, j: (0, 0))],
        out_specs=tok(D_MODEL),
        out_shape=jax.ShapeDtypeStruct((n, D_MODEL), F32),
        scratch_shapes=[pltpu.VMEM((tn, P_HK), F32), pltpu.VMEM((tn, P_HK), F32), pltpu.VMEM((tn, P_HK), F32),
                        pltpu.VMEM((tn * W_PITCH, P_NKEYS), F32), pltpu.VMEM((tn, D_MODEL), F32),
                        pltpu.VMEM((2 * P_TOPK, tn), F32), pltpu.VMEM((2 * P_TOPK, tn), F32),
                        pltpu.VMEM((P_HK, tn), F32), pltpu.VMEM((P_HK, tn), F32)],
        compiler_params=_cparams(("parallel", "arbitrary")),
        name="peer",
    )(x1, xn, qp, subkeys, u, v, gain)


def _split_w_in(w):
    widths = [2 * M_HEADS * M_DQK, M_HEADS * M_DV, M_HEADS, M_HEADS, M_HEADS * M_DV, N_HEADS * N_HD] + \
             [N_KV * N_HD] * 6 + [3 * N_HEADS, D_MODEL, D_MODEL]
    names = ['m_qk', 'm_v', 'm_i', 'm_f', 'm_o', 'n_q', 'n_kc', 'n_vc', 'n_ks', 'n_vs', 'n_kw', 'n_vw', 'n_g', 'g_a', 'g_b']
    out, off = {}, 0
    for nme, wd in zip(names, widths):
        out[nme] = w[:, off:off + wd]
        off += wd
    return out


def _stage_a(x_prompt, x_sample, past, norm_mix, w_in):
    B, T, D = x_prompt.shape
    BS, TS, _ = x_sample.shape
    xs_pad = jnp.pad(x_sample, ((0, 0), (0, SAMPLE_PAD - TS), (0, 0)))
    x_all = jnp.concatenate([x_prompt.reshape(B * T, D), xs_pad.reshape(BS * SAMPLE_PAD, D)], axis=0)
    rows = x_all.shape[0]
    rows_pad = -(-rows // ROW_TILE) * ROW_TILE
    x_all = jnp.pad(x_all, ((0, rows_pad - rows), (0, 0)))
    pos = jnp.concatenate([jnp.tile(jnp.arange(T, dtype=jnp.int32), B),
                           jnp.tile(past + jnp.arange(SAMPLE_PAD, dtype=jnp.int32), BS),
                           jnp.zeros((rows_pad - rows,), jnp.int32)])
    cos_t, sin_t = _rope_tables(pos)
    cols = _split_w_in(w_in)
    w_main = jnp.concatenate([cols[k] for k in ('m_qk', 'm_v', 'm_o', 'n_q', 'n_kc', 'n_vc', 'n_ks', 'n_vs',
                                                'n_kw', 'n_vw', 'g_a', 'g_b')], axis=1).astype(MXU_DTYPE)
    w_small = jnp.concatenate([cols['m_i'], cols['m_f'], cols['n_g'],
                               jnp.zeros((D, SMALL_W - 2 * M_HEADS - 3 * N_HEADS), w_in.dtype)], axis=1).astype(MXU_DTYPE)
    z, zs = _inproj(x_all, norm_mix.reshape(1, D), w_main, w_small, cos_t, sin_t)
    return x_all, z, zs


def kernel(x_prompt, x_sample, cache_cmp_k, cache_cmp_v, cache_slc_k, cache_slc_v, state_win_k, state_win_v, state_conv, state_C, state_n, state_m, page_table, norm_mix, w_in, m_conv_w, m_conv_b, m_bias_i, m_bias_f, m_norm, cmp_k_w1, cmp_k_b1, cmp_k_w2, cmp_v_w1, cmp_v_b1, cmp_v_w2, w_out, norm_ffn, peer_wq, peer_subkeys, peer_u, peer_v, norm_final):
    B, T, D = x_prompt.shape
    BS, TS, _ = x_sample.shape
    past = page_table.shape[1] * PAGE_SIZE
    assert TS <= SAMPLE_PAD and past >= WINDOW and T >= WIN_SPAN
    assert T % MLSTM_ROWS == 0 and T % (2 * SLC_TILE) == 0 and T % NSA_TILE == 0
    assert (past // PAGE_SIZE) % TRANSPOSE_UNROLL == 0
    n_rows_p, n_rows_s = B * T, BS * SAMPLE_PAD
    x_all, z, zs = _stage_a(x_prompt, x_sample, past, norm_mix[0], w_in[0])
    zeros = lambda *s: jnp.zeros(s, F32)

    brow = jnp.concatenate([m_bias_i[0], m_bias_f[0], zeros(SMALL_W - 2 * M_HEADS)]).reshape(1, -1)
    norm = m_norm[0].reshape(1, -1)
    cb = m_conv_b[0].reshape(1, -1)
    qk_w = 2 * M_HEADS * M_DQK
    ha_p, C_p, n_p, m_p = _mlstm(z, zs, 0, B, T, MLSTM_ROWS, M_CHUNK, M_CHUNK, zeros(B, 8, qk_w), m_conv_w[0], cb, brow,
                                 norm, zeros(B, M_HEADS, M_DQK, M_DV), zeros(B, M_HEADS, 1, M_DQK),
                                 zeros(B, M_HEADS, 1, 128))
    hist = jnp.pad(state_conv[0], ((0, 0), (8 - (M_CONV - 1), 0), (0, 0)))
    m0 = jnp.broadcast_to(state_m[0][:, :, None, None], (BS, M_HEADS, 1, 128))
    ha_s, C_s, n_s, m_s = _mlstm(z, zs, n_rows_p, BS, SAMPLE_PAD, SAMPLE_PAD, SAMPLE_PAD, TS, hist, m_conv_w[0], cb,
                                 brow, norm, state_C[0], state_n[0][:, :, None, :], m0)

    sect = lambda off, r0, r1: z[r0:r1, off:off + KV_W]
    wk = _compress_weights(cmp_k_w1[0], cmp_k_b1[0], cmp_k_w2[0])
    wv = _compress_weights(cmp_v_w1[0], cmp_v_b1[0], cmp_v_w2[0])
    up128 = lambda n: -(-n // 128) * 128
    nbc_p = T // CMP_BLOCK
    n_sel_p = T // SLC_BLOCK
    kcmp_p, vcmp_p = sect(OFF_KC, 0, n_rows_p), sect(OFF_VC, 0, n_rows_p)
    kc_p = _compress_prompt(z, B, T, OFF_KC, wk, _block_tables(nbc_p, True))
    vc_p = _compress_prompt(z, B, T, OFF_VC, wv, _block_tables(nbc_p, False))
    prompt_sec = {off: sect(off, 0, n_rows_p) for off in (OFF_KS, OFF_VS, OFF_KW, OFF_VW)}
    heads_first = lambda a: a.reshape(B, T, N_KV, N_HD).transpose(0, 2, 1, 3).astype(MXU_DTYPE)
    ks_h, vs_h, kw_h, vw_h = [heads_first(prompt_sec[o]) for o in (OFF_KS, OFF_VS, OFF_KW, OFF_VW)]
    ones_col = (jnp.arange(N_HD) == 0).astype(MXU_DTYPE)
    vs_aug = jnp.concatenate([vs_h, jnp.broadcast_to(ones_col, vs_h.shape)], axis=-1)
    hb_p = _nsa_prompt(z, zs, kc_p, vc_p, ks_h, vs_aug, kw_h, vw_h, B, T, nbc_p, n_sel_p, up128(n_sel_p))

    nb_past = past // CMP_BLOCK
    nbc_s = nb_past + SLC_BLOCK // CMP_BLOCK
    n_sel_s = nbc_s * CMP_BLOCK // SLC_BLOCK
    nbp_s = up128(nb_past + 8)
    row_blk_s = n_rows_p // SAMPLE_PAD
    pool_t = lambda pool: pool[0].transpose(0, 2, 3, 1).reshape(-1, KV_W, PAGE_SIZE)
    kc_s = _compress_sample(page_table, pool_t(cache_cmp_k), z, row_blk_s, OFF_KC // KV_W, wk,
                            _block_tables(nbp_s, True), nbp_s, TS)
    vc_s = _compress_sample(page_table, pool_t(cache_cmp_v), z, row_blk_s, OFF_VC // KV_W, wv,
                            _block_tables(nbp_s, False), nbp_s, TS)
    n_win = state_win_k.shape[2]
    hb_s = _nsa_sample(page_table, z, zs, row_blk_s, kc_s, vc_s, state_win_k[0].reshape(BS, n_win, KV_W),
                       state_win_v[0].reshape(BS, n_win, KV_W), pool_t(cache_slc_k), pool_t(cache_slc_v),
                       TS, nbc_s, n_sel_s, up128(n_sel_s))

    wo = w_out[0].astype(MXU_DTYPE)
    wq = peer_wq[0].astype(MXU_DTYPE)
    gain_ffn = norm_ffn[0].reshape(1, D)
    sk = peer_subkeys[0].reshape(2 * P_HEADS, P_NKEYS, P_DKEY // 2).astype(MXU_DTYPE)
    pu, pv = peer_u[0].astype(MXU_DTYPE), peer_v[0].astype(MXU_DTYPE)
    gain_out = norm_final.reshape(1, D)
    ys = []
    for ha, hb, row0 in ((ha_p, hb_p, 0), (ha_s, hb_s, n_rows_p)):
        x1, xn, qp = _merge(x_all, z, ha, hb, row0, wo, gain_ffn, wq)
        ys.append(_peer(x1, xn, qp, sk, pu, pv, gain_out))
    y_prompt = ys[0].reshape(B, T, D)
    y_sample = ys[1].reshape(BS, SAMPLE_PAD, D)[:, :TS]

    kvshape_p = (1, B, T, N_KV, N_HD)
    zsam = z[n_rows_p:n_rows_p + n_rows_s].reshape(BS, SAMPLE_PAD, Z_MAIN)[:, :TS]
    new_s = lambda off: zsam[:, :, off:off + KV_W].reshape(1, BS, TS, N_KV, N_HD)
    n_win_p = min(WINDOW, T)
    win_p = lambda off: prompt_sec[off].reshape(kvshape_p)[:, :, T - n_win_p:]
    win_s = lambda state, off: jnp.concatenate([state, new_s(off)], axis=2)[:, :, -n_win:]
    conv_p = z[:n_rows_p, OFF_MQ:OFF_MQ + qk_w].reshape(1, B, T, qk_w)[:, :, T - (M_CONV - 1):]
    conv_s = jnp.concatenate([state_conv, zsam[None, :, :, OFF_MQ:OFF_MQ + qk_w]], axis=2)[:, :, -(M_CONV - 1):]
    return (y_prompt, y_sample,
            kcmp_p.reshape(kvshape_p), new_s(OFF_KC), vcmp_p.reshape(kvshape_p), new_s(OFF_VC),
            prompt_sec[OFF_KS].reshape(kvshape_p), new_s(OFF_KS), prompt_sec[OFF_VS].reshape(kvshape_p), new_s(OFF_VS),
            win_p(OFF_KW), win_s(state_win_k, OFF_KW), win_p(OFF_VW), win_s(state_win_v, OFF_VW),
            conv_p, conv_s, C_p[None], C_s[None], n_p[None, :, :, 0], n_s[None, :, :, 0],
            m_p[None, :, :, 0, 0], m_s[None, :, :, 0, 0])
```

```python
import functools

import jax
import jax.numpy as jnp
from jax import lax
from jax.experimental import pallas as pl
from jax.experimental.pallas import tpu as pltpu

F32 = jnp.float32
MXU_DTYPE = jnp.bfloat16
HIGHEST = lax.Precision.HIGHEST
EPS = 1e-6
NEG = -1e30

D_MODEL = 1024
PAGE_SIZE = 128
M_HEADS, M_DQK, M_DV, M_CONV, M_CHUNK = 4, 128, 256, 4, 64
N_HEADS, N_KV, N_GROUP, N_HD = 16, 4, 4, 64
CMP_BLOCK, CMP_HIDDEN, SLC_BLOCK, SLC_TOPK, WINDOW, Q_BLOCK = 32, 128, 64, 16, 512, 128
ROPE_THETA, ROPE_DIM = 500000.0, 16
P_HEADS, P_NKEYS, P_DKEY, P_TOPK = 8, 128, 256, 16

OFF_MQ, OFF_MK, OFF_MV, OFF_MO, OFF_NQ = 0, 512, 1024, 2048, 3072
OFF_KC, OFF_VC, OFF_KS, OFF_VS, OFF_KW, OFF_VW = 4096, 4352, 4608, 4864, 5120, 5376
OFF_GA, OFF_GB, Z_MAIN = 5632, 6656, 7680
SMALL_W = 128
SM_I, SM_F, SM_G = 0, 4, 8
ROW_TILE = 1024
MLSTM_ROWS = 512
COL_TILE = 512
SAMPLE_PAD = 8
VMEM_LIMIT = 56 * 1024 * 1024


def _cparams(sem):
    return pltpu.CompilerParams(dimension_semantics=sem, vmem_limit_bytes=VMEM_LIMIT)


def _mm(a, b):
    return jnp.dot(a.astype(MXU_DTYPE), b.astype(MXU_DTYPE), preferred_element_type=F32)


def _mm_nt(a, b):
    return lax.dot_general(a.astype(MXU_DTYPE), b.astype(MXU_DTYPE), (((1,), (1,)), ((), ())),
                           preferred_element_type=F32)


def _mm_tn(a, b):
    return lax.dot_general(a.astype(MXU_DTYPE), b.astype(MXU_DTYPE), (((0,), (0,)), ((), ())),
                           preferred_element_type=F32)


def _log_sigmoid(x):
    return jnp.minimum(x, 0.0) - jnp.log(1.0 + jnp.exp(-jnp.abs(x)))


def _sigmoid(x):
    return 1.0 / (1.0 + jnp.exp(-x))


def _col_to_row(x_col):
    L = x_col.shape[0]
    ti = lax.broadcasted_iota(jnp.int32, (L, L), 0)
    si = lax.broadcasted_iota(jnp.int32, (L, L), 1)
    return jnp.sum(jnp.where(ti == si, x_col, 0.0), axis=0, keepdims=True)


def _inproj_kernel(x_ref, g_ref, w_ref, ws_ref, cos_ref, sin_ref, z_ref, zs_ref, xn_scr):
    j = pl.program_id(1)

    @pl.when(j == 0)
    def _():
        x = x_ref[...]
        r = x * lax.rsqrt(jnp.mean(x * x, axis=-1, keepdims=True) + EPS) * g_ref[...]
        xn = r.astype(MXU_DTYPE)
        xn_scr[...] = xn
        zs_ref[...] = jnp.dot(xn, ws_ref[...], preferred_element_type=F32)

    acc = jnp.dot(xn_scr[...], w_ref[...], preferred_element_type=F32)
    rope_full = (j == OFF_NQ // COL_TILE) | (j == OFF_NQ // COL_TILE + 1)
    rope_half = (j == OFF_KS // COL_TILE) | (j == OFF_KW // COL_TILE)
    is_rope = rope_full | rope_half

    @pl.when(is_rope)
    def _():
        lane = lax.broadcasted_iota(jnp.int32, acc.shape, 1)
        on = lane < jnp.where(rope_full, COL_TILE, COL_TILE // 2)
        reps = COL_TILE // cos_ref.shape[1]
        cos = jnp.where(on, jnp.concatenate([cos_ref[...]] * reps, axis=1), 1.0)
        sin = jnp.where(on, jnp.concatenate([sin_ref[...]] * reps, axis=1), 0.0)
        half = ROPE_DIM // 2
        partner = jnp.where(lane % N_HD < half, pltpu.roll(acc, COL_TILE - half, 1), pltpu.roll(acc, half, 1))
        z_ref[...] = acc * cos + partner * sin

    @pl.when(jnp.logical_not(is_rope))
    def _():
        z_ref[...] = acc


def _inproj(x_all, gain, w_main, w_small, cos_t, sin_t):
    rows = x_all.shape[0]
    grid = (rows // ROW_TILE, Z_MAIN // COL_TILE)
    return pl.pallas_call(
        _inproj_kernel,
        grid=grid,
        in_specs=[
            pl.BlockSpec((ROW_TILE, D_MODEL), lambda i, j: (i, 0)),
            pl.BlockSpec((1, D_MODEL), lambda i, j: (0, 0)),
            pl.BlockSpec((D_MODEL, COL_TILE), lambda i, j: (0, j)),
            pl.BlockSpec((D_MODEL, SMALL_W), lambda i, j: (0, 0)),
            pl.BlockSpec((ROW_TILE, 128), lambda i, j: (i, 0)),
            pl.BlockSpec((ROW_TILE, 128), lambda i, j: (i, 0)),
        ],
        out_specs=[
            pl.BlockSpec((ROW_TILE, COL_TILE), lambda i, j: (i, j)),
            pl.BlockSpec((ROW_TILE, SMALL_W), lambda i, j: (i, 0)),
        ],
        out_shape=[
            jax.ShapeDtypeStruct((rows, Z_MAIN), F32),
            jax.ShapeDtypeStruct((rows, SMALL_W), F32),
        ],
        scratch_shapes=[pltpu.VMEM((ROW_TILE, D_MODEL), MXU_DTYPE)],
        compiler_params=_cparams(("parallel", "arbitrary")),
        name="inproj",
    )(x_all, gain, w_main, w_small, cos_t, sin_t)


def _rope_tables(pos):
    half = ROPE_DIM // 2
    inv = ROPE_THETA ** (-jnp.arange(half, dtype=F32) / half)
    ang = pos.astype(F32)[:, None] * inv
    cos, sin = jnp.cos(ang), jnp.sin(ang)
    rows = pos.shape[0]
    pad1 = jnp.ones((rows, N_HD - ROPE_DIM), F32)
    pad0 = jnp.zeros((rows, N_HD - ROPE_DIM), F32)
    cos_h = jnp.concatenate([cos, cos, pad1], axis=1)
    sin_h = jnp.concatenate([-sin, sin, pad0], axis=1)
    return jnp.concatenate([cos_h, cos_h], axis=1), jnp.concatenate([sin_h, sin_h], axis=1)


def _mlstm_kernel(zq_ref, zk_ref, zv_ref, zo_ref, zs_ref, hq_ref, hk_ref, cwq_ref, cwk_ref, cbq_ref, cbk_ref,
                  brow_ref, norm_ref, c0_ref, n0_ref, m0_ref,
                  h_ref, c_ref, n_ref, m_ref, c_scr, n_scr, m_scr, hq_scr, hk_scr, *, chunk, nvalid):
    t = pl.program_id(2)
    h = pl.program_id(1)
    rb = zq_ref.shape[0]

    @pl.when(t == 0)
    def _():
        c_scr[...] = c0_ref[0, 0]
        n_scr[...] = n0_ref[0, 0]
        m_scr[...] = m0_ref[0, 0]
        hq_scr[...] = hq_ref[0]
        hk_scr[...] = hk_ref[0]

    def conv_silu(z_ref_, hist_scr, cw_ref, cb_ref):
        z = z_ref_[...]
        ext = jnp.concatenate([hist_scr[...], z], axis=0)
        acc = cb_ref[...]
        for jj in range(M_CONV):
            off = 8 - (M_CONV - 1) + jj
            acc = acc + ext[off:off + rb] * cw_ref[jj:jj + 1, :]
        hist_scr[...] = z[rb - 8:rb]
        return acc * _sigmoid(acc)

    q_all = conv_silu(zq_ref, hq_scr, cwq_ref, cbq_ref)
    k_all = conv_silu(zk_ref, hk_scr, cwk_ref, cbk_ref) * (M_DQK ** -0.5)

    zs = zs_ref[...] + brow_ref[...]
    col = lax.broadcasted_iota(jnp.int32, (rb, SMALL_W), 1)
    li_cols = jnp.sum(jnp.where(col == SM_I + h, zs, 0.0), axis=1, keepdims=True)
    lf_cols = _log_sigmoid(jnp.sum(jnp.where(col == SM_F + h, zs, 0.0), axis=1, keepdims=True))

    L = chunk
    ti = lax.broadcasted_iota(jnp.int32, (L, L), 0)
    si = lax.broadcasted_iota(jnp.int32, (L, L), 1)
    mask = (si <= ti) & (si < nvalid)
    tril = (si <= ti).astype(F32)
    tcol = lax.broadcasted_iota(jnp.int32, (L, 1), 0)

    C = c_scr[...]
    n = n_scr[...]
    m = m_scr[:, 0:1]
    gain = norm_ref[...]
    for c in range(rb // L):
        sl = slice(c * L, (c + 1) * L)
        q, k, v = q_all[sl], k_all[sl], zv_ref[sl, :]
        li_c = li_cols[sl]
        lf_c = jnp.where(tcol < nvalid, lf_cols[sl], 0.0)
        b_c = jnp.dot(tril, jnp.broadcast_to(lf_c, (L, 128)), precision=HIGHEST, preferred_element_type=F32)[:, 0:1]
        b_r = _col_to_row(b_c)
        li_r = _col_to_row(li_c)
        logw = jnp.where(mask, b_c - b_r + li_r, NEG)
        inter = b_c + m
        m_t = jnp.maximum(inter, jnp.max(logw, axis=1, keepdims=True))
        w = jnp.where(mask, jnp.exp(logw - m_t), 0.0) * _mm_nt(q, k)
        w_inter = jnp.exp(inter - m_t)
        num = _mm(w, v) + w_inter * _mm(q, C)
        den = jnp.sum(w, axis=1, keepdims=True) + w_inter * jnp.sum(q * n, axis=1, keepdims=True)
        hh = num / jnp.maximum(jnp.abs(den), jnp.exp(-m_t))
        b_end = b_c[L - 1:L]
        m_new = m_t[L - 1:L]
        w_end = jnp.where(tcol < nvalid, jnp.exp(b_end - b_c + li_c - m_new), 0.0)
        decay = jnp.exp(b_end + m - m_new)
        kw = w_end * k
        C = decay * C + _mm_tn(kw, v)
        n = decay * n + jnp.sum(kw, axis=0, keepdims=True)
        m = m_new
        hh = hh * lax.rsqrt(jnp.mean(hh * hh, axis=-1, keepdims=True) + EPS) * gain
        h_ref[sl, :] = _sigmoid(zo_ref[sl, :]) * hh

    c_scr[...] = C
    n_scr[...] = n
    m_scr[...] = jnp.broadcast_to(m, m_scr.shape)

    @pl.when(t == pl.num_programs(2) - 1)
    def _():
        c_ref[0, 0] = C
        n_ref[0, 0] = n
        m_ref[0, 0] = jnp.broadcast_to(m, m_scr.shape)


def _mlstm(z, zs, row0, nb, rows_per_seq, rb, chunk, nvalid, hist, conv_w, conv_b, brow, norm, c0, n0, m0):
    assert row0 % rb == 0 and rows_per_seq % rb == 0 and rb % chunk == 0
    nt = rows_per_seq // rb
    rblk = lambda b, t: row0 // rb + b * nt + t
    kern = functools.partial(_mlstm_kernel, chunk=chunk, nvalid=nvalid)
    return pl.pallas_call(
        kern,
        grid=(nb, M_HEADS, nt),
        in_specs=[
            pl.BlockSpec((rb, M_DQK), lambda b, h, t: (rblk(b, t), OFF_MQ // M_DQK + h)),
            pl.BlockSpec((rb, M_DQK), lambda b, h, t: (rblk(b, t), OFF_MK // M_DQK + h)),
            pl.BlockSpec((rb, M_DV), lambda b, h, t: (rblk(b, t), OFF_MV // M_DV + h)),
            pl.BlockSpec((rb, M_DV), lambda b, h, t: (rblk(b, t), OFF_MO // M_DV + h)),
            pl.BlockSpec((rb, SMALL_W), lambda b, h, t: (rblk(b, t), 0)),
            pl.BlockSpec((1, 8, M_DQK), lambda b, h, t: (b, 0, OFF_MQ // M_DQK + h)),
            pl.BlockSpec((1, 8, M_DQK), lambda b, h, t: (b, 0, OFF_MK // M_DQK + h)),
            pl.BlockSpec((M_CONV, M_DQK), lambda b, h, t: (0, h)),
            pl.BlockSpec((M_CONV, M_DQK), lambda b, h, t: (0, M_HEADS + h)),
            pl.BlockSpec((1, M_DQK), lambda b, h, t: (0, h)),
            pl.BlockSpec((1, M_DQK), lambda b, h, t: (0, M_HEADS + h)),
            pl.BlockSpec((1, SMALL_W), lambda b, h, t: (0, 0)),
            pl.BlockSpec((1, M_DV), lambda b, h, t: (0, h)),
            pl.BlockSpec((1, 1, M_DQK, M_DV), lambda b, h, t: (b, h, 0, 0)),
            pl.BlockSpec((1, 1, 1, M_DQK), lambda b, h, t: (b, h, 0, 0)),
            pl.BlockSpec((1, 1, 1, 128), lambda b, h, t: (b, h, 0, 0)),
        ],
        out_specs=[
            pl.BlockSpec((rb, M_DV), lambda b, h, t: (b * nt + t, h)),
            pl.BlockSpec((1, 1, M_DQK, M_DV), lambda b, h, t: (b, h, 0, 0)),
            pl.BlockSpec((1, 1, 1, M_DQK), lambda b, h, t: (b, h, 0, 0)),
            pl.BlockSpec((1, 1, 1, 128), lambda b, h, t: (b, h, 0, 0)),
        ],
        out_shape=[
            jax.ShapeDtypeStruct((nb * rows_per_seq, M_HEADS * M_DV), F32),
            jax.ShapeDtypeStruct((nb, M_HEADS, M_DQK, M_DV), F32),
            jax.ShapeDtypeStruct((nb, M_HEADS, 1, M_DQK), F32),
            jax.ShapeDtypeStruct((nb, M_HEADS, 1, 128), F32),
        ],
        scratch_shapes=[
            pltpu.VMEM((M_DQK, M_DV), F32),
            pltpu.VMEM((1, M_DQK), F32),
            pltpu.VMEM((1, 128), F32),
            pltpu.VMEM((8, M_DQK), F32),
            pltpu.VMEM((8, M_DQK), F32),
        ],
        compiler_params=_cparams(("parallel", "parallel", "arbitrary")),
        name="mlstm",
    )(z, z, z, z, zs, hist, hist, conv_w, conv_w, conv_b, conv_b, brow, norm, c0, n0, m0)


KV_W = N_KV * N_HD
HID_W = N_KV * CMP_HIDDEN
TRANSPOSE_UNROLL = 4
BLOCK_PITCH = CMP_BLOCK + 4


def _gelu(x):
    return 0.5 * x * (1.0 + lax.erf(x * (2.0 ** -0.5)))


def _rope_rows(x, cos, sin):
    lane = lax.broadcasted_iota(jnp.int32, x.shape, 1)
    half = ROPE_DIM // 2
    w = x.shape[1]
    partner = jnp.where(lane % N_HD < half, pltpu.roll(x, w - half, 1), pltpu.roll(x, half, 1))
    reps = w // cos.shape[1]
    return x * jnp.concatenate([cos] * reps, axis=1) + partner * jnp.concatenate([sin] * reps, axis=1)


def _store_heads(o_ref, row0, val):
    for kv in range(N_KV):
        o_ref[0, kv, row0:row0 + val.shape[0], :] = val[:, kv * N_HD:(kv + 1) * N_HD]


def _compress_blocks(lo_ref, hi_ref, nb, w1_ref, b1_ref, w2_ref, pitch=CMP_BLOCK):
    parts = []
    for j in range(CMP_BLOCK):
        parts.append(lo_ref[pl.ds(j, nb, stride=pitch), :].astype(MXU_DTYPE))
        parts.append(hi_ref[pl.ds(j, nb, stride=pitch), :].astype(MXU_DTYPE))
    hid = jnp.dot(jnp.concatenate(parts, axis=1), w1_ref[...], preferred_element_type=F32)
    return _mm(_gelu(hid + b1_ref[...]), w2_ref[...])


def _compress_prompt_kernel(lo_ref, hi_ref, w1_ref, b1_ref, w2_ref, cos_ref, sin_ref, o_ref, *, nb):
    out = _compress_blocks(lo_ref, hi_ref, nb, w1_ref, b1_ref, w2_ref)
    _store_heads(o_ref, 0, _rope_rows(out, cos_ref[0:nb, :], sin_ref[0:nb, :]))


def _compress_prompt(z, B, T, col_off, wts, tabs):
    nb = T // CMP_BLOCK
    cos, sin = tabs
    w1, b1, w2 = wts
    kern = functools.partial(_compress_prompt_kernel, nb=nb)
    c2 = lambda b: (0, 0)
    return pl.pallas_call(
        kern,
        grid=(B,),
        in_specs=[pl.BlockSpec((T, 128), lambda b: (b, col_off // 128)),
                  pl.BlockSpec((T, 128), lambda b: (b, col_off // 128 + 1)),
                  pl.BlockSpec((CMP_BLOCK * KV_W, HID_W), c2),
                  pl.BlockSpec((1, HID_W), c2), pl.BlockSpec((HID_W, KV_W), c2),
                  pl.BlockSpec(cos.shape, c2), pl.BlockSpec(sin.shape, c2)],
        out_specs=pl.BlockSpec((1, N_KV, nb, N_HD), lambda b: (b, 0, 0, 0)),
        out_shape=jax.ShapeDtypeStruct((B, N_KV, nb, N_HD), F32),
        compiler_params=_cparams(("parallel",)),
        name="compress_prompt",
    )(z, z, w1, b1, w2, cos, sin)


def _page_t_copy(pt_ref, b, p, pool_ref, buf_ref, sem):
    return pltpu.make_async_copy(pool_ref.at[pt_ref[b, p]],
                                 buf_ref.at[:, pl.ds(pl.multiple_of(p * PAGE_SIZE, PAGE_SIZE), PAGE_SIZE)], sem)


def _compress_sample_kernel(pt_ref, poolk_ref, poolv_ref, new_ref, w1_ref, b1_ref, w2_ref, cos_ref, sin_ref, o_ref,
                            stage2, lo_buf, hi_buf, sem, *, n_pages, nb, n_new):
    c = pl.program_id(0)
    b = pl.program_id(1)
    nseq = pl.num_programs(1)
    step = c * nseq + b
    slot = step % 2

    def gather(which, seq, sl, go):
        pool_ref = poolk_ref if which == 0 else poolv_ref

        def one(p, carry):
            cp = _page_t_copy(pt_ref, seq, p, pool_ref, stage2.at[sl], sem.at[sl])
            if go:
                cp.start()
            else:
                cp.wait()
            return carry
        lax.fori_loop(0, n_pages, one, 0)

    def gather_step(s, sl, go):
        @pl.when(s < nseq)
        def _():
            gather(0, s, sl, go)

        @pl.when(s >= nseq)
        def _():
            gather(1, s - nseq, sl, go)

    @pl.when(step == 0)
    def _():
        gather_step(step, slot, True)

    @pl.when(step + 1 < 2 * nseq)
    def _():
        gather_step(step + 1, 1 - slot, True)

    stage = stage2.at[slot]

    blocks_per_page = PAGE_SIZE // CMP_BLOCK

    def to_rows(p4, c):
        for u in range(TRANSPOSE_UNROLL):
            p = p4 * TRANSPOSE_UNROLL + u
            page = stage[:, pl.ds(pl.multiple_of(p * PAGE_SIZE, PAGE_SIZE), PAGE_SIZE)]
            base = pl.multiple_of(p * (blocks_per_page * BLOCK_PITCH), 8)
            for half, buf in ((0, lo_buf), (1, hi_buf)):
                rows = page[half * 128:(half + 1) * 128, :].T
                for bl in range(blocks_per_page):
                    buf[pl.ds(base + bl * BLOCK_PITCH, CMP_BLOCK), :] = rows[bl * CMP_BLOCK:(bl + 1) * CMP_BLOCK]
        return c

    w1, b1, w2 = w1_ref.at[0], b1_ref.at[0], w2_ref.at[0]
    cos, sin, out_ref = cos_ref.at[0], sin_ref.at[0], o_ref.at[0, 0]
    out_ref[...] = jnp.zeros(out_ref.shape, out_ref.dtype)
    xn = new_ref[...]
    hid = jnp.zeros((8, HID_W), F32)
    for jrow in range(n_new):
        hid = hid + _mm(jnp.broadcast_to(xn[jrow:jrow + 1, :], (8, KV_W)), w1[jrow * KV_W:(jrow + 1) * KV_W, :])
    row = lax.broadcasted_iota(jnp.int32, (8, HID_W), 0)
    hid = jnp.where(row == 0, hid, 0.0) + b1[...]
    new = _mm(_gelu(hid), w2[...])
    out_ref[nb:nb + 8, :] = _rope_rows(new, cos[nb:nb + 8, :], sin[nb:nb + 8, :])
    gather_step(step, slot, False)
    lax.fori_loop(0, n_pages // TRANSPOSE_UNROLL, to_rows, 0)
    out = _compress_blocks(lo_buf, hi_buf, nb, w1, b1, w2, pitch=BLOCK_PITCH)
    out_ref[0:nb, :] = _rope_rows(out, cos[0:nb, :], sin[0:nb, :])


def _compress_sample(page_table, pool_k, pool_v, z, new_row_blk0, new_col_blk, wts_k, wts_v, tabs_k, tabs_v, nbp, n_new):
    BS, n_pages = page_table.shape
    nb = n_pages * PAGE_SIZE // CMP_BLOCK
    stack = lambda a, b: jnp.stack([a, b])
    cos, sin = stack(tabs_k[0], tabs_v[0]), stack(tabs_k[1], tabs_v[1])
    w1, b1, w2 = (stack(a, b) for a, b in zip(wts_k, wts_v))
    kern = functools.partial(_compress_sample_kernel, n_pages=n_pages, nb=nb, n_new=n_new)
    per_pool = lambda shp: pl.BlockSpec((1,) + shp, lambda c, b, pt: (c, 0, 0))
    gs = pltpu.PrefetchScalarGridSpec(
        num_scalar_prefetch=1,
        grid=(2, BS),
        in_specs=[pl.BlockSpec(memory_space=pl.ANY), pl.BlockSpec(memory_space=pl.ANY),
                  pl.BlockSpec((8, KV_W), lambda c, b, pt: (new_row_blk0 + b, new_col_blk + c)),
                  per_pool((CMP_BLOCK * KV_W, HID_W)), per_pool((1, HID_W)), per_pool((HID_W, KV_W)),
                  per_pool((nbp, 128)), per_pool((nbp, 128))],
        out_specs=pl.BlockSpec((1, 1, nbp, KV_W), lambda c, b, pt: (c, b, 0, 0)),
        scratch_shapes=[pltpu.VMEM((2, KV_W, n_pages * PAGE_SIZE), F32),
                        pltpu.VMEM((nb * BLOCK_PITCH, 128), F32), pltpu.VMEM((nb * BLOCK_PITCH, 128), F32),
                        pltpu.SemaphoreType.DMA((2,))],
    )
    return pl.pallas_call(
        kern,
        grid_spec=gs,
        out_shape=jax.ShapeDtypeStruct((2, BS, nbp, KV_W), F32),
        compiler_params=_cparams(("arbitrary", "arbitrary")),
        name="compress_sample",
    )(page_table, pool_k, pool_v, z, w1, b1, w2, cos, sin)


def _compress_weights(w1, b1, w2):
    eye = jnp.eye(N_KV, dtype=w1.dtype)
    w1big = jnp.einsum('jdh,kl->jkdlh', w1, eye).reshape(CMP_BLOCK * KV_W, HID_W)
    w2big = jnp.einsum('hd,kl->khld', w2, eye).reshape(HID_W, KV_W)
    return w1big.astype(MXU_DTYPE), jnp.tile(b1, N_KV).reshape(1, HID_W).astype(F32), w2big.astype(MXU_DTYPE)


def _block_tables(nbp, rope):
    cos, sin = _rope_tables(jnp.arange(nbp, dtype=jnp.int32) * CMP_BLOCK + CMP_BLOCK - 1)
    if not rope:
        cos, sin = jnp.ones_like(cos), jnp.zeros_like(sin)
    return cos, sin


SCALE = N_HD ** -0.5
INF = float('inf')


def _masked_softmax(s, mask):
    s = jnp.where(mask, s, NEG)
    m = jnp.max(s, axis=1, keepdims=True)
    e = jnp.where(mask, jnp.exp(s - m), 0.0)
    return e / jnp.maximum(jnp.sum(e, axis=1, keepdims=True), 1.0)


def _topk_mask(score, k):
    st = score.T
    rid = lax.broadcasted_iota(jnp.int32, st.shape, 0).astype(F32)

    def body(r, carry):
        s, sel = carry
        m = jnp.max(s, axis=0, keepdims=True)
        idx = jnp.min(jnp.where(s == m, rid, float(st.shape[0])), axis=0, keepdims=True)
        hit = rid == idx
        return jnp.where(hit, -INF, s), jnp.where(hit, 1.0, sel)

    _, sel = lax.fori_loop(0, k, body, (st, jnp.zeros(st.shape, F32)))
    return sel.T


def _stack_heads(q, tile):
    return jnp.concatenate([q[:, g * N_HD:(g + 1) * N_HD] for g in range(N_GROUP)], axis=0)


def _compressed_branch(qs, kc, vc, tq, nbc, wp, tile):
    nbp = kc.shape[0]
    blk = lax.broadcasted_iota(jnp.int32, (1, nbp), 1)
    s_c = _mm_nt(qs, kc)
    mask_c = (blk * CMP_BLOCK + CMP_BLOCK - 1 <= tq) & (blk < nbc)
    p_c = _masked_softmax(s_c, mask_c)
    o_cmp = _mm(p_c, vc)
    psum = p_c[0:tile]
    for g in range(1, N_GROUP):
        psum = psum + p_c[g * tile:(g + 1) * tile]
    ratio = SLC_BLOCK // CMP_BLOCK
    pair = jnp.where(lax.broadcasted_iota(jnp.int32, (nbp, wp), 0) // ratio
                     == lax.broadcasted_iota(jnp.int32, (nbp, wp), 1), 1.0, 0.0)
    return o_cmp, jnp.dot(psum, pair, precision=HIGHEST, preferred_element_type=F32)


def _select_blocks(imp, tq_t, n_sel, wp):
    sb = lax.broadcasted_iota(jnp.int32, (1, wp), 1)
    valid = (sb * SLC_BLOCK <= tq_t) & (sb < n_sel)
    forced = (sb == 0) | (sb == tq_t // SLC_BLOCK)
    score = jnp.where(valid, jnp.where(forced, INF, imp), -INF)
    return _topk_mask(score, min(SLC_TOPK, n_sel))


def _expand_sel(sel, kpos, wp):
    sbc = lax.broadcasted_iota(jnp.int32, (wp, 1), 0)
    e = jnp.where(sbc == kpos // SLC_BLOCK, 1.0, 0.0)
    return jnp.dot(sel.astype(jnp.bfloat16), e.astype(jnp.bfloat16), preferred_element_type=F32)


def _online_step(carry, s, mask, v):
    m, l, acc = carry
    s = jnp.where(mask, s, NEG)
    m_new = jnp.maximum(m, jnp.max(s, axis=1, keepdims=True))
    alpha = jnp.exp(m - m_new)
    p = jnp.where(mask, jnp.exp(s - m_new), 0.0)
    return m_new, alpha * l + jnp.sum(p, axis=1, keepdims=True), alpha * acc + _mm(p, v)


def _gate_col(gates, idx):
    col = lax.broadcasted_iota(jnp.int32, gates.shape, 1)
    return jnp.sum(jnp.where(col == idx, gates, 0.0), axis=1, keepdims=True)


def _combine(gates, head0, o_cmp, o_slc, o_win, tile):
    outs = []
    for g in range(N_GROUP):
        rows = slice(g * tile, (g + 1) * tile)
        c0 = SM_G + (head0 + g) * 3
        outs.append(_gate_col(gates, c0) * o_cmp[rows] + _gate_col(gates, c0 + 1) * o_slc[rows]
                    + _gate_col(gates, c0 + 2) * o_win[rows])
    return jnp.concatenate(outs, axis=1)


SLC_TILE = 512
DECODE_CHUNK = 2048
NSA_TILE = 256
WIN_SPAN = WINDOW + NSA_TILE


def _nsa_prompt_kernel(q_ref, zs_ref, kc_ref, vc_ref, ks_ref, vs_ref, kw_ref, vw_ref, o_ref, *, nbc, n_sel, wp):
    kv = pl.program_id(1)
    i = pl.program_id(2)
    tile = NSA_TILE
    qs = _stack_heads(q_ref[...] * SCALE, tile).astype(MXU_DTYPE)
    trow = lax.broadcasted_iota(jnp.int32, (N_GROUP * tile, 1), 0) % tile
    tq = i * tile + trow
    tq_t = i * tile + lax.broadcasted_iota(jnp.int32, (tile, 1), 0)

    o_cmp, imp = _compressed_branch(qs, kc_ref[0, 0], vc_ref[0, 0], tq, nbc, wp, tile)
    sel = _select_blocks(imp, tq_t, n_sel, wp).astype(jnp.bfloat16)

    def one_tile(t, carry):
        m, acc = carry
        k0 = pl.multiple_of(t * SLC_TILE, SLC_TILE)
        kpos = k0 + lax.broadcasted_iota(jnp.int32, (1, SLC_TILE), 1)
        ok = (_expand_sel(sel, kpos, wp) > 0.5) & (kpos <= tq_t)
        bias = jnp.where(ok, 0.0, NEG)
        s = _mm_nt(qs, ks_ref[0, 0, pl.ds(k0, SLC_TILE), :]) + jnp.concatenate([bias] * N_GROUP, axis=0)
        m_new = jnp.maximum(m, jnp.max(s, axis=1, keepdims=True))
        p = jnp.exp(s - m_new)
        return m_new, jnp.exp(m - m_new) * acc + _mm(p, vs_ref[0, 0, pl.ds(k0, SLC_TILE), :])

    def slc_step(j, carry):
        return one_tile(2 * j + 1, one_tile(2 * j, carry))

    rows = N_GROUP * tile
    init = (jnp.full((rows, 1), NEG, F32), jnp.zeros((rows, 2 * N_HD), F32))
    n_pairs = (i * tile + tile - 1) // (2 * SLC_TILE) + 1
    _, acc = lax.fori_loop(0, n_pairs, slc_step, init)
    o_slc = acc[:, :N_HD] / jnp.maximum(acc[:, N_HD:N_HD + 1], 1.0)

    w0 = pl.multiple_of(jnp.maximum(i * tile - WINDOW, 0), tile)
    pw = w0 + lax.broadcasted_iota(jnp.int32, (1, WIN_SPAN), 1)
    dist = tq_t - pw
    bias_w = jnp.where((dist >= 0) & (dist < WINDOW), 0.0, NEG)
    s_w = _mm_nt(qs, kw_ref[0, 0, pl.ds(w0, WIN_SPAN), :]) + jnp.concatenate([bias_w] * N_GROUP, axis=0)
    p_w = jnp.exp(s_w - jnp.max(s_w, axis=1, keepdims=True))
    acc_w = _mm(p_w, vw_ref[0, 0, pl.ds(w0, WIN_SPAN), :])
    o_win = acc_w[:, :N_HD] / jnp.maximum(acc_w[:, N_HD:N_HD + 1], 1.0)

    o_ref[...] = _combine(_sigmoid(zs_ref[...]), kv * N_GROUP, o_cmp, o_slc, o_win, tile)


def _nsa_prompt(z, zs, kc, vc, ks, vs, kw, vw, B, T, nbc, n_sel, wp):
    nq = T // NSA_TILE
    nbp = kc.shape[2]
    kern = functools.partial(_nsa_prompt_kernel, nbc=nbc, n_sel=n_sel, wp=wp)
    seq = lambda b, kv, i: (b, kv, 0, 0)
    return pl.pallas_call(
        kern,
        grid=(B, N_KV, nq),
        in_specs=[pl.BlockSpec((NSA_TILE, KV_W), lambda b, kv, i: (b * nq + i, OFF_NQ // KV_W + kv)),
                  pl.BlockSpec((NSA_TILE, SMALL_W), lambda b, kv, i: (b * nq + i, 0)),
                  pl.BlockSpec((1, 1, nbp, N_HD), seq), pl.BlockSpec((1, 1, nbp, N_HD), seq),
                  pl.BlockSpec((1, 1, T, N_HD), seq), pl.BlockSpec((1, 1, T, 2 * N_HD), seq),
                  pl.BlockSpec((1, 1, T, N_HD), seq), pl.BlockSpec((1, 1, T, 2 * N_HD), seq)],
        out_specs=pl.BlockSpec((NSA_TILE, KV_W), lambda b, kv, i: (b * nq + i, kv)),
        out_shape=jax.ShapeDtypeStruct((B * T, N_HEADS * N_HD), F32),
        compiler_params=_cparams(("parallel", "parallel", "arbitrary")),
        name="nsa_prompt",
    )(z, zs, kc, vc, ks, vs, kw, vw)


def _nsa_sample_kernel(pt_ref, q_ref, zs_ref, kc_ref, vc_ref, ksn_ref, vsn_ref, kwn_ref, vwn_ref, wk_ref, wv_ref,
                       e_hbm, pk_ref, pv_ref, o_ref, kbuf2, vbuf2, sem_k, sem_v, e_ref, sem_e,
                       *, past, n_new, nbc, n_sel, wp, chunk):
    b = pl.program_id(0)
    n_pages = past // PAGE_SIZE
    tile = SAMPLE_PAD
    rows = N_GROUP * tile
    slot = b % 2

    def gather(seq, sl, go):
        def one(p, c):
            ck = _page_t_copy(pt_ref, seq, p, pk_ref, kbuf2.at[sl], sem_k.at[sl])
            cv = _page_t_copy(pt_ref, seq, p, pv_ref, vbuf2.at[sl], sem_v.at[sl])
            if go:
                ck.start()
                cv.start()
            else:
                ck.wait()
                cv.wait()
            return c
        lax.fori_loop(0, n_pages, one, 0)

    @pl.when(b == 0)
    def _():
        gather(b, slot, True)
        load_e = pltpu.make_async_copy(e_hbm, e_ref, sem_e)
        load_e.start()
        load_e.wait()

    @pl.when(b + 1 < pl.num_programs(0))
    def _():
        gather(b + 1, 1 - slot, True)

    kbuf, vbuf = kbuf2.at[slot], vbuf2.at[slot]

    q8 = q_ref[...] * SCALE
    lane = lax.broadcasted_iota(jnp.int32, (tile, KV_W), 1)
    qrows = []
    for kv in range(N_KV):
        sec = q8[:, kv * KV_W:(kv + 1) * KV_W]
        for g in range(N_GROUP):
            moved = sec if g == kv else pltpu.roll(sec, ((kv - g) % N_KV) * N_HD, 1)
            qrows.append(jnp.where(lane // N_HD == kv, moved, 0.0))
    qbd = jnp.concatenate(qrows, axis=0).astype(MXU_DTYPE)

    def own_head(full):
        return jnp.concatenate([full[kv * rows:(kv + 1) * rows, kv * N_HD:(kv + 1) * N_HD] for kv in range(N_KV)], axis=0)

    nrow = N_KV * rows
    tq = past + lax.broadcasted_iota(jnp.int32, (nrow, 1), 0) % tile
    newpos = past + lax.broadcasted_iota(jnp.int32, (1, tile), 1)
    new_ok = (newpos < past + n_new) & (newpos <= tq)

    kc, vc = kc_ref[0], vc_ref[0]
    nbp = kc.shape[0]
    blk = lax.broadcasted_iota(jnp.int32, (1, nbp), 1)
    p_c = _masked_softmax(_mm_nt(qbd, kc), (blk * CMP_BLOCK + CMP_BLOCK - 1 <= tq) & (blk < nbc))
    o_cmp = own_head(_mm(p_c, vc))
    psum = []
    for kv in range(N_KV):
        acc = p_c[kv * rows:kv * rows + tile]
        for g in range(1, N_GROUP):
            acc = acc + p_c[kv * rows + g * tile:kv * rows + (g + 1) * tile]
        psum.extend([acc] * N_GROUP)
    ratio = SLC_BLOCK // CMP_BLOCK
    pair = jnp.where(lax.broadcasted_iota(jnp.int32, (nbp, wp), 0) // ratio
                     == lax.broadcasted_iota(jnp.int32, (nbp, wp), 1), 1.0, 0.0)
    imp = jnp.dot(jnp.concatenate(psum, axis=0), pair, precision=HIGHEST, preferred_element_type=F32)
    sel_rows = _select_blocks(imp, tq, n_sel, wp).astype(jnp.bfloat16)

    wpos = past - WINDOW + lax.broadcasted_iota(jnp.int32, (1, WINDOW), 1)
    wdist = tq - wpos
    win_ok = (wdist >= 0) & (wdist < WINDOW) & (wpos >= 0)
    s1 = jnp.where(win_ok, _mm_nt(qbd, wk_ref[0]), NEG)
    s2 = jnp.where(new_ok, _mm_nt(qbd, kwn_ref[...]), NEG)
    m = jnp.maximum(jnp.max(s1, axis=1, keepdims=True), jnp.max(s2, axis=1, keepdims=True))
    e1 = jnp.where(win_ok, jnp.exp(s1 - m), 0.0)
    e2 = jnp.where(new_ok, jnp.exp(s2 - m), 0.0)
    den = jnp.maximum(jnp.sum(e1, axis=1, keepdims=True) + jnp.sum(e2, axis=1, keepdims=True), 1.0)
    o_win = own_head(_mm(e1, wv_ref[0]) + _mm(e2, vwn_ref[...])) / den

    gather(b, slot, False)

    carry = (jnp.full((nrow, 1), NEG, F32), jnp.zeros((nrow, 1), F32), jnp.zeros((nrow, KV_W), F32))
    for c in range(past // chunk):
        cs = slice(c * chunk, (c + 1) * chunk)
        s = jnp.dot(qbd, kbuf[:, cs].astype(MXU_DTYPE), preferred_element_type=F32)
        hit = jnp.dot(sel_rows, e_ref[:, cs], preferred_element_type=F32) > 0.5
        mm, l, acc = carry
        s = jnp.where(hit, s, NEG)
        m_new = jnp.maximum(mm, jnp.max(s, axis=1, keepdims=True))
        alpha = jnp.exp(mm - m_new)
        p = jnp.where(hit, jnp.exp(s - m_new), 0.0)
        carry = (m_new, alpha * l + jnp.sum(p, axis=1, keepdims=True), alpha * acc + _mm_nt(p, vbuf[:, cs]))
    hit_new = (_expand_sel(sel_rows, newpos, wp) > 0.5) & new_ok
    _, l, acc = _online_step(carry, _mm_nt(qbd, ksn_ref[...]), hit_new, vsn_ref[...])
    o_slc = own_head(acc) / jnp.maximum(l, 1.0)

    gates = _sigmoid(zs_ref[...])
    gcol = lambda c: jnp.concatenate([gates[:, SM_G + hh * 3 + c:SM_G + hh * 3 + c + 1] for hh in range(N_HEADS)], axis=0)
    comb = gcol(0) * o_cmp + gcol(1) * o_slc + gcol(2) * o_win
    o_ref[...] = jnp.concatenate([comb[hh * tile:(hh + 1) * tile] for hh in range(N_HEADS)], axis=1)


def _nsa_sample(page_table, z, zs, row_blk0, kc, vc, win_k, win_v, pool_k, pool_v, n_new, nbc, n_sel, wp):
    BS, n_pages = page_table.shape
    past = n_pages * PAGE_SIZE
    chunk = min(past, DECODE_CHUNK)
    kern = functools.partial(_nsa_sample_kernel, past=past, n_new=n_new, nbc=nbc, n_sel=n_sel, wp=wp, chunk=chunk)
    new = lambda off: pl.BlockSpec((SAMPLE_PAD, KV_W), lambda b, pt: (row_blk0 + b, off // KV_W))
    seq3 = lambda b, pt: (b, 0, 0)
    expand = (jnp.arange(wp, dtype=jnp.int32)[:, None] == jnp.arange(past, dtype=jnp.int32)[None, :] // SLC_BLOCK
              ).astype(jnp.bfloat16)
    gs = pltpu.PrefetchScalarGridSpec(
        num_scalar_prefetch=1,
        grid=(BS,),
        in_specs=[pl.BlockSpec((SAMPLE_PAD, N_HEADS * N_HD), lambda b, pt: (row_blk0 + b, OFF_NQ // (N_HEADS * N_HD))),
                  pl.BlockSpec((SAMPLE_PAD, SMALL_W), lambda b, pt: (row_blk0 + b, 0)),
                  pl.BlockSpec((1, kc.shape[1], KV_W), seq3), pl.BlockSpec((1, kc.shape[1], KV_W), seq3),
                  new(OFF_KS), new(OFF_VS), new(OFF_KW), new(OFF_VW),
                  pl.BlockSpec((1, WINDOW, KV_W), seq3), pl.BlockSpec((1, WINDOW, KV_W), seq3),
                  pl.BlockSpec(memory_space=pl.ANY),
                  pl.BlockSpec(memory_space=pl.ANY), pl.BlockSpec(memory_space=pl.ANY)],
        out_specs=pl.BlockSpec((SAMPLE_PAD, N_HEADS * N_HD), lambda b, pt: (b, 0)),
        scratch_shapes=[pltpu.VMEM((2, KV_W, past), F32), pltpu.VMEM((2, KV_W, past), F32),
                        pltpu.SemaphoreType.DMA((2,)), pltpu.SemaphoreType.DMA((2,)),
                        pltpu.VMEM((wp, past), jnp.bfloat16), pltpu.SemaphoreType.DMA(())],
    )
    return pl.pallas_call(
        kern,
        grid_spec=gs,
        out_shape=jax.ShapeDtypeStruct((BS * SAMPLE_PAD, N_HEADS * N_HD), F32),
        compiler_params=_cparams(("arbitrary",)),
        name="nsa_sample",
    )(page_table, z, zs, kc, vc, z, z, z, z, win_k, win_v, expand, pool_k, pool_v)


MERGE_TILE = 256
P_QW = P_HEADS * P_DKEY
P_EXPERTS = P_NKEYS * P_NKEYS
P_HK = P_HEADS * P_TOPK


def _merge_kernel(x_ref, ha_ref, hb_ref, ga0_ref, ga1_ref, gb0_ref, gb1_ref, wo_ref, gain_ref, wq_ref,
                  x1_ref, xn_ref, qp_ref):
    ga = _sigmoid(jnp.concatenate([ga0_ref[...], ga1_ref[...]], axis=1))
    gb = _sigmoid(jnp.concatenate([gb0_ref[...], gb1_ref[...]], axis=1))
    x1 = x_ref[...] + _mm(ga * ha_ref[...] + gb * hb_ref[...], wo_ref[...])
    x1_ref[...] = x1
    xn = (x1 * lax.rsqrt(jnp.mean(x1 * x1, axis=-1, keepdims=True) + EPS) * gain_ref[...]).astype(MXU_DTYPE)
    xn_ref[...] = xn
    qp_ref[...] = jnp.dot(xn, wq_ref[...], preferred_element_type=F32)


def _merge(x_all, z, ha, hb, row0, w_out, gain, wq):
    n = ha.shape[0]
    tm = MERGE_TILE
    assert n % tm == 0 and row0 % tm == 0
    r0 = row0 // tm
    zc = lambda off: pl.BlockSpec((tm, COL_TILE), lambda i: (r0 + i, off // COL_TILE))
    full = lambda shp: pl.BlockSpec(shp, lambda i: (0, 0))
    loc = lambda w: pl.BlockSpec((tm, w), lambda i: (i, 0))
    return pl.pallas_call(
        _merge_kernel,
        grid=(n // tm,),
        in_specs=[pl.BlockSpec((tm, D_MODEL), lambda i: (r0 + i, 0)), loc(D_MODEL), loc(D_MODEL),
                  zc(OFF_GA), zc(OFF_GA + COL_TILE), zc(OFF_GB), zc(OFF_GB + COL_TILE),
                  full((D_MODEL, D_MODEL)), full((1, D_MODEL)), full((D_MODEL, P_QW))],
        out_specs=[loc(D_MODEL), loc(D_MODEL), loc(P_QW)],
        out_shape=[jax.ShapeDtypeStruct((n, D_MODEL), F32), jax.ShapeDtypeStruct((n, D_MODEL), MXU_DTYPE),
                   jax.ShapeDtypeStruct((n, P_QW), F32)],
        compiler_params=_cparams(("parallel",)),
        name="merge",
    )(x_all, ha, hb, z, z, z, z, w_out, gain, wq)


PEER_TILE = 256
PEER_EB = 1024
PEER_SLABS = PEER_EB // P_NKEYS
W_PITCH = P_NKEYS + 4


def _pop_topk_cols(scores, k, emits):
    def body(r, ss):
        out = []
        for s, emit in zip(ss, emits):
            rid = lax.broadcasted_iota(jnp.int32, s.shape, 0).astype(F32)
            m = jnp.max(s, axis=0, keepdims=True)
            idx = jnp.min(jnp.where(s == m, rid, float(s.shape[0])), axis=0, keepdims=True)
            hit = rid == idx
            emit(r, m, idx, hit)
            out.append(jnp.where(hit, -INF, s))
        return tuple(out)

    lax.fori_loop(0, k, body, tuple(scores))


_CAND_GROUPS = [(0, 0), (0, 8)] + [(r0, 0) for r0 in range(1, P_TOPK)]


def _peer_select(qp_ref, sk_ref, a_scr, b_scr, g_scr, v_scr, i_scr, e_scr, bv_scr):
    tn = qp_ref.shape[0]
    sub8 = lax.broadcasted_iota(jnp.int32, (8, tn), 0)
    for h in range(P_HEADS):
        scores, emits = [], []
        for c in range(2):
            hc = 2 * h + c
            scores.append(_mm_nt(sk_ref[hc], qp_ref[:, hc * P_NKEYS:(hc + 1) * P_NKEYS]))

            def emit(r, m, idx, hit, hc=hc):
                v_scr[pl.ds(hc * P_TOPK + r, 1), :] = m
                i_scr[pl.ds(hc * P_TOPK + r, 1), :] = idx

            emits.append(emit)
        _pop_topk_cols(scores, P_TOPK, emits)
    for h0 in range(0, P_HEADS, 2):
        cands, emits = [], []
        for h in (h0, h0 + 1):
            o0, o1 = 2 * h * P_TOPK, (2 * h + 1) * P_TOPK
            cand, cexp = [], []
            for r0, r1 in _CAND_GROUPS:
                ok = (r0 + 1) * (r1 + sub8 + 1) <= P_TOPK
                cand.append(jnp.where(ok, v_scr[o0 + r0:o0 + r0 + 1, :] + v_scr[o1 + r1:o1 + r1 + 8, :], -INF))
                cexp.append(i_scr[o0 + r0:o0 + r0 + 1, :] * float(P_NKEYS) + i_scr[o1 + r1:o1 + r1 + 8, :])
            cands.append(jnp.concatenate(cand, axis=0))
            cexp = jnp.concatenate(cexp, axis=0)

            def emit2(r, m, idx, hit, h=h, cexp=cexp):
                e_scr[pl.ds(h * P_TOPK + r, 1), :] = jnp.sum(jnp.where(hit, cexp, 0.0), axis=0, keepdims=True)
                bv_scr[pl.ds(h * P_TOPK + r, 1), :] = m

            emits.append(emit2)
        _pop_topk_cols(cands, P_TOPK, emits)
    gates = []
    for h in range(P_HEADS):
        best = bv_scr[h * P_TOPK:(h + 1) * P_TOPK, :]
        ex = jnp.exp(best - best[0:1, :])
        gates.append(ex / jnp.sum(ex, axis=0, keepdims=True))
    e = e_scr[...]
    a = jnp.floor(e * (1.0 / P_NKEYS))
    a_scr[...] = a.T
    b_scr[...] = (e - a * float(P_NKEYS)).T
    g_scr[...] = jnp.concatenate(gates, axis=0).T


def _peer_kernel(x1_ref, xn_ref, qp_ref, sk_ref, u_ref, v_ref, gain_ref, y_ref,
                 a_scr, b_scr, g_scr, w_scr, acc_scr, v_scr, i_scr, e_scr, bv_scr):
    j = pl.program_id(1)
    tn = x1_ref.shape[0]

    @pl.when(j == 0)
    def _():
        _peer_select(qp_ref, sk_ref, a_scr, b_scr, g_scr, v_scr, i_scr, e_scr, bv_scr)
        sub = lax.broadcasted_iota(jnp.int32, (P_NKEYS, P_HK), 0).astype(F32)

        def per_group(n8, c):
            r0 = pl.multiple_of(n8 * 8, 8)
            a8, b8, g8 = a_scr[pl.ds(r0, 8), :], b_scr[pl.ds(r0, 8), :], g_scr[pl.ds(r0, 8), :]
            for t in range(8):
                at = jnp.where(sub == a8[t:t + 1], g8[t:t + 1], 0.0)
                bt = jnp.where(sub == b8[t:t + 1], 1.0, 0.0)
                w_scr[pl.ds(pl.multiple_of(r0 * W_PITCH, 8) + t * W_PITCH, P_NKEYS), :] = _mm_nt(at, bt)
            return c

        lax.fori_loop(0, tn // 8, per_group, 0)
        acc_scr[...] = jnp.zeros(acc_scr.shape, F32)

    act = _gelu(_mm_nt(xn_ref[...], u_ref[...]))
    parts = []
    for r in range(PEER_SLABS):
        w = w_scr[pl.ds(j * PEER_SLABS + r, tn, stride=W_PITCH), :]
        parts.append(w * act[:, r * P_NKEYS:(r + 1) * P_NKEYS])
    acc_scr[...] += _mm(jnp.concatenate(parts, axis=1), v_ref[...])

    @pl.when(j == pl.num_programs(1) - 1)
    def _():
        x2 = x1_ref[...] + acc_scr[...]
        y_ref[...] = x2 * lax.rsqrt(jnp.mean(x2 * x2, axis=-1, keepdims=True) + EPS) * gain_ref[...]


def _peer(x1, xn, qp, subkeys, u, v, gain):
    n = x1.shape[0]
    tn = PEER_TILE
    assert n % tn == 0
    tok = lambda w: pl.BlockSpec((tn, w), lambda i, j: (i, 0))
    return pl.pallas_call(
        _peer_kernel,
        grid=(n // tn, P_EXPERTS // PEER_EB),
        in_specs=[tok(D_MODEL), tok(D_MODEL), tok(P_QW),
                  pl.BlockSpec((2 * P_HEADS, P_NKEYS, P_DKEY // 2), lambda i, j: (0, 0, 0)),
                  pl.BlockSpec((PEER_EB, D_MODEL), lambda i, j: (j, 0)),
                  pl.BlockSpec((PEER_EB, D_MODEL), lambda i, j: (j, 0)),
                  pl.BlockSpec((1, D_MODEL), lambda i, j: (0, 0))],
        out_specs=tok(D_MODEL),
        out_shape=jax.ShapeDtypeStruct((n, D_MODEL), F32),
        scratch_shapes=[pltpu.VMEM((tn, P_HK), F32), pltpu.VMEM((tn, P_HK), F32), pltpu.VMEM((tn, P_HK), F32),
                        pltpu.VMEM((tn * W_PITCH, P_NKEYS), F32), pltpu.VMEM((tn, D_MODEL), F32),
                        pltpu.VMEM((2 * P_HEADS * P_TOPK, tn), F32), pltpu.VMEM((2 * P_HEADS * P_TOPK, tn), F32),
                        pltpu.VMEM((P_HK, tn), F32), pltpu.VMEM((P_HK, tn), F32)],
        compiler_params=_cparams(("parallel", "arbitrary")),
        name="peer",
    )(x1, xn, qp, subkeys, u, v, gain)


def _split_w_in(w):
    widths = [2 * M_HEADS * M_DQK, M_HEADS * M_DV, M_HEADS, M_HEADS, M_HEADS * M_DV, N_HEADS * N_HD] + \
             [N_KV * N_HD] * 6 + [3 * N_HEADS, D_MODEL, D_MODEL]
    names = ['m_qk', 'm_v', 'm_i', 'm_f', 'm_o', 'n_q', 'n_kc', 'n_vc', 'n_ks', 'n_vs', 'n_kw', 'n_vw', 'n_g', 'g_a', 'g_b']
    out, off = {}, 0
    for nme, wd in zip(names, widths):
        out[nme] = w[:, off:off + wd]
        off += wd
    return out


def _stage_a(x_prompt, x_sample, past, norm_mix, w_in):
    B, T, D = x_prompt.shape
    BS, TS, _ = x_sample.shape
    xs_pad = jnp.pad(x_sample, ((0, 0), (0, SAMPLE_PAD - TS), (0, 0)))
    x_all = jnp.concatenate([x_prompt.reshape(B * T, D), xs_pad.reshape(BS * SAMPLE_PAD, D)], axis=0)
    rows = x_all.shape[0]
    rows_pad = -(-rows // ROW_TILE) * ROW_TILE
    x_all = jnp.pad(x_all, ((0, rows_pad - rows), (0, 0)))
    pos = jnp.concatenate([jnp.tile(jnp.arange(T, dtype=jnp.int32), B),
                           jnp.tile(past + jnp.arange(SAMPLE_PAD, dtype=jnp.int32), BS),
                           jnp.zeros((rows_pad - rows,), jnp.int32)])
    cos_t, sin_t = _rope_tables(pos)
    cols = _split_w_in(w_in)
    w_main = jnp.concatenate([cols[k] for k in ('m_qk', 'm_v', 'm_o', 'n_q', 'n_kc', 'n_vc', 'n_ks', 'n_vs',
                                                'n_kw', 'n_vw', 'g_a', 'g_b')], axis=1).astype(MXU_DTYPE)
    w_small = jnp.concatenate([cols['m_i'], cols['m_f'], cols['n_g'],
                               jnp.zeros((D, SMALL_W - 2 * M_HEADS - 3 * N_HEADS), w_in.dtype)], axis=1).astype(MXU_DTYPE)
    z, zs = _inproj(x_all, norm_mix.reshape(1, D), w_main, w_small, cos_t, sin_t)
    return x_all, z, zs


def kernel(x_prompt, x_sample, cache_cmp_k, cache_cmp_v, cache_slc_k, cache_slc_v, state_win_k, state_win_v, state_conv, state_C, state_n, state_m, page_table, norm_mix, w_in, m_conv_w, m_conv_b, m_bias_i, m_bias_f, m_norm, cmp_k_w1, cmp_k_b1, cmp_k_w2, cmp_v_w1, cmp_v_b1, cmp_v_w2, w_out, norm_ffn, peer_wq, peer_subkeys, peer_u, peer_v, norm_final):
    B, T, D = x_prompt.shape
    BS, TS, _ = x_sample.shape
    past = page_table.shape[1] * PAGE_SIZE
    assert TS <= SAMPLE_PAD and past >= WINDOW and T >= WIN_SPAN
    assert T % MLSTM_ROWS == 0 and T % (2 * SLC_TILE) == 0 and T % NSA_TILE == 0
    assert (past // PAGE_SIZE) % TRANSPOSE_UNROLL == 0
    n_rows_p, n_rows_s = B * T, BS * SAMPLE_PAD
    x_all, z, zs = _stage_a(x_prompt, x_sample, past, norm_mix[0], w_in[0])
    zeros = lambda *s: jnp.zeros(s, F32)

    brow = jnp.concatenate([m_bias_i[0], m_bias_f[0], zeros(SMALL_W - 2 * M_HEADS)]).reshape(1, -1)
    norm = m_norm[0].reshape(1, -1)
    cb = m_conv_b[0].reshape(1, -1)
    qk_w = 2 * M_HEADS * M_DQK
    ha_p, C_p, n_p, m_p = _mlstm(z, zs, 0, B, T, MLSTM_ROWS, M_CHUNK, M_CHUNK, zeros(B, 8, qk_w), m_conv_w[0], cb, brow,
                                 norm, zeros(B, M_HEADS, M_DQK, M_DV), zeros(B, M_HEADS, 1, M_DQK),
                                 zeros(B, M_HEADS, 1, 128))
    hist = jnp.pad(state_conv[0], ((0, 0), (8 - (M_CONV - 1), 0), (0, 0)))
    m0 = jnp.broadcast_to(state_m[0][:, :, None, None], (BS, M_HEADS, 1, 128))
    ha_s, C_s, n_s, m_s = _mlstm(z, zs, n_rows_p, BS, SAMPLE_PAD, SAMPLE_PAD, SAMPLE_PAD, TS, hist, m_conv_w[0], cb,
                                 brow, norm, state_C[0], state_n[0][:, :, None, :], m0)

    sect = lambda off, r0, r1: z[r0:r1, off:off + KV_W]
    wk = _compress_weights(cmp_k_w1[0], cmp_k_b1[0], cmp_k_w2[0])
    wv = _compress_weights(cmp_v_w1[0], cmp_v_b1[0], cmp_v_w2[0])
    up128 = lambda n: -(-n // 128) * 128
    nbc_p = T // CMP_BLOCK
    n_sel_p = T // SLC_BLOCK
    kcmp_p, vcmp_p = sect(OFF_KC, 0, n_rows_p), sect(OFF_VC, 0, n_rows_p)
    kc_p = _compress_prompt(z, B, T, OFF_KC, wk, _block_tables(nbc_p, True))
    vc_p = _compress_prompt(z, B, T, OFF_VC, wv, _block_tables(nbc_p, False))
    prompt_sec = {off: sect(off, 0, n_rows_p) for off in (OFF_KS, OFF_VS, OFF_KW, OFF_VW)}
    heads_first = lambda a: a.reshape(B, T, N_KV, N_HD).transpose(0, 2, 1, 3).astype(MXU_DTYPE)
    ks_h, vs_h, kw_h, vw_h = [heads_first(prompt_sec[o]) for o in (OFF_KS, OFF_VS, OFF_KW, OFF_VW)]
    ones_col = (jnp.arange(N_HD) == 0).astype(MXU_DTYPE)
    aug = lambda v: jnp.concatenate([v, jnp.broadcast_to(ones_col, v.shape)], axis=-1)
    hb_p = _nsa_prompt(z, zs, kc_p, vc_p, ks_h, aug(vs_h), kw_h, aug(vw_h), B, T, nbc_p, n_sel_p, up128(n_sel_p))

    nb_past = past // CMP_BLOCK
    nbc_s = nb_past + SLC_BLOCK // CMP_BLOCK
    n_sel_s = nbc_s * CMP_BLOCK // SLC_BLOCK
    nbp_s = up128(nb_past + 8)
    row_blk_s = n_rows_p // SAMPLE_PAD
    pool_t = lambda pool: pool[0].transpose(0, 2, 3, 1).reshape(-1, KV_W, PAGE_SIZE)
    assert OFF_VC == OFF_KC + KV_W
    kvc_s = _compress_sample(page_table, pool_t(cache_cmp_k), pool_t(cache_cmp_v), z, row_blk_s, OFF_KC // KV_W, wk, wv,
                             _block_tables(nbp_s, True), _block_tables(nbp_s, False), nbp_s, TS)
    kc_s, vc_s = kvc_s[0], kvc_s[1]
    n_win = state_win_k.shape[2]
    hb_s = _nsa_sample(page_table, z, zs, row_blk_s, kc_s, vc_s, state_win_k[0].reshape(BS, n_win, KV_W),
                       state_win_v[0].reshape(BS, n_win, KV_W), pool_t(cache_slc_k), pool_t(cache_slc_v),
                       TS, nbc_s, n_sel_s, up128(n_sel_s))

    wo = w_out[0].astype(MXU_DTYPE)
    wq = peer_wq[0].astype(MXU_DTYPE)
    gain_ffn = norm_ffn[0].reshape(1, D)
    sk = peer_subkeys[0].reshape(2 * P_HEADS, P_NKEYS, P_DKEY // 2).astype(MXU_DTYPE)
    pu, pv = peer_u[0].astype(MXU_DTYPE), peer_v[0].astype(MXU_DTYPE)
    gain_out = norm_final.reshape(1, D)
    ys = []
    for ha, hb, row0 in ((ha_p, hb_p, 0), (ha_s, hb_s, n_rows_p)):
        x1, xn, qp = _merge(x_all, z, ha, hb, row0, wo, gain_ffn, wq)
        ys.append(_peer(x1, xn, qp, sk, pu, pv, gain_out))
    y_prompt = ys[0].reshape(B, T, D)
    y_sample = ys[1].reshape(BS, SAMPLE_PAD, D)[:, :TS]

    kvshape_p = (1, B, T, N_KV, N_HD)
    zsam = z[n_rows_p:n_rows_p + n_rows_s].reshape(BS, SAMPLE_PAD, Z_MAIN)[:, :TS]
    new_s = lambda off: zsam[:, :, off:off + KV_W].reshape(1, BS, TS, N_KV, N_HD)
    n_win_p = min(WINDOW, T)
    win_p = lambda off: prompt_sec[off].reshape(kvshape_p)[:, :, T - n_win_p:]
    win_s = lambda state, off: jnp.concatenate([state, new_s(off)], axis=2)[:, :, -n_win:]
    conv_p = z[:n_rows_p, OFF_MQ:OFF_MQ + qk_w].reshape(1, B, T, qk_w)[:, :, T - (M_CONV - 1):]
    conv_s = jnp.concatenate([state_conv, zsam[None, :, :, OFF_MQ:OFF_MQ + qk_w]], axis=2)[:, :, -(M_CONV - 1):]
    return (y_prompt, y_sample,
            kcmp_p.reshape(kvshape_p), new_s(OFF_KC), vcmp_p.reshape(kvshape_p), new_s(OFF_VC),
            prompt_sec[OFF_KS].reshape(kvshape_p), new_s(OFF_KS), prompt_sec[OFF_VS].reshape(kvshape_p), new_s(OFF_VS),
            win_p(OFF_KW), win_s(state_win_k, OFF_KW), win_p(OFF_VW), win_s(state_win_v, OFF_VW),
            conv_p, conv_s, C_p[None], C_s[None], n_p[None, :, :, 0], n_s[None, :, :, 0],
            m_p[None, :, :, 0, 0], m_s[None, :, :, 0, 0])
```

```python
import functools

import jax
import jax.numpy as jnp
from jax import lax
from jax.experimental import pallas as pl
from jax.experimental.pallas import tpu as pltpu

F32 = jnp.float32
MXU_DTYPE = jnp.bfloat16
HIGHEST = lax.Precision.HIGHEST
EPS = 1e-6
NEG = -1e30

D_MODEL = 1024
PAGE_SIZE = 128
M_HEADS, M_DQK, M_DV, M_CONV, M_CHUNK = 4, 128, 256, 4, 64
N_HEADS, N_KV, N_GROUP, N_HD = 16, 4, 4, 64
CMP_BLOCK, CMP_HIDDEN, SLC_BLOCK, SLC_TOPK, WINDOW, Q_BLOCK = 32, 128, 64, 16, 512, 128
ROPE_THETA, ROPE_DIM = 500000.0, 16
P_HEADS, P_NKEYS, P_DKEY, P_TOPK = 8, 128, 256, 16

OFF_MQ, OFF_MK, OFF_MV, OFF_MO, OFF_NQ = 0, 512, 1024, 2048, 3072
OFF_KC, OFF_VC, OFF_KS, OFF_VS, OFF_KW, OFF_VW = 4096, 4352, 4608, 4864, 5120, 5376
OFF_GA, OFF_GB, Z_MAIN = 5632, 6656, 7680
SMALL_W = 128
SM_I, SM_F, SM_G = 0, 4, 8
ROW_TILE = 1024
MLSTM_ROWS = 512
MLSTM_CHUNK = 256
COL_TILE = 512
SAMPLE_PAD = 8
VMEM_LIMIT = 56 * 1024 * 1024


def _cparams(sem):
    return pltpu.CompilerParams(dimension_semantics=sem, vmem_limit_bytes=VMEM_LIMIT)


def _mm(a, b):
    return jnp.dot(a.astype(MXU_DTYPE), b.astype(MXU_DTYPE), preferred_element_type=F32)


def _mm_nt(a, b):
    return lax.dot_general(a.astype(MXU_DTYPE), b.astype(MXU_DTYPE), (((1,), (1,)), ((), ())),
                           preferred_element_type=F32)


def _mm_tn(a, b):
    return lax.dot_general(a.astype(MXU_DTYPE), b.astype(MXU_DTYPE), (((0,), (0,)), ((), ())),
                           preferred_element_type=F32)


def _log_sigmoid(x):
    return jnp.minimum(x, 0.0) - jnp.log(1.0 + jnp.exp(-jnp.abs(x)))


def _sigmoid(x):
    return 1.0 / (1.0 + jnp.exp(-x))


def _col_to_row(x_col):
    L = x_col.shape[0]
    ti = lax.broadcasted_iota(jnp.int32, (L, L), 0)
    si = lax.broadcasted_iota(jnp.int32, (L, L), 1)
    return jnp.sum(jnp.where(ti == si, x_col, 0.0), axis=0, keepdims=True)


def _inproj_kernel(x_ref, g_ref, w_ref, ws_ref, cos_ref, sin_ref, z_ref, zs_ref, xn_scr):
    j = pl.program_id(1)

    @pl.when(j == 0)
    def _():
        x = x_ref[...]
        r = x * lax.rsqrt(jnp.mean(x * x, axis=-1, keepdims=True) + EPS) * g_ref[...]
        xn = r.astype(MXU_DTYPE)
        xn_scr[...] = xn
        zs_ref[...] = jnp.dot(xn, ws_ref[...], preferred_element_type=F32)

    acc = jnp.dot(xn_scr[...], w_ref[...], preferred_element_type=F32)
    rope_full = (j == OFF_NQ // COL_TILE) | (j == OFF_NQ // COL_TILE + 1)
    rope_half = (j == OFF_KS // COL_TILE) | (j == OFF_KW // COL_TILE)
    is_rope = rope_full | rope_half

    @pl.when(is_rope)
    def _():
        lane = lax.broadcasted_iota(jnp.int32, acc.shape, 1)
        on = lane < jnp.where(rope_full, COL_TILE, COL_TILE // 2)
        reps = COL_TILE // cos_ref.shape[1]
        cos = jnp.where(on, jnp.concatenate([cos_ref[...]] * reps, axis=1), 1.0)
        sin = jnp.where(on, jnp.concatenate([sin_ref[...]] * reps, axis=1), 0.0)
        half = ROPE_DIM // 2
        partner = jnp.where(lane % N_HD < half, pltpu.roll(acc, COL_TILE - half, 1), pltpu.roll(acc, half, 1))
        z_ref[...] = acc * cos + partner * sin

    @pl.when(jnp.logical_not(is_rope))
    def _():
        z_ref[...] = acc


def _inproj(x_all, gain, w_main, w_small, cos_t, sin_t):
    rows = x_all.shape[0]
    grid = (rows // ROW_TILE, Z_MAIN // COL_TILE)
    return pl.pallas_call(
        _inproj_kernel,
        grid=grid,
        in_specs=[
            pl.BlockSpec((ROW_TILE, D_MODEL), lambda i, j: (i, 0)),
            pl.BlockSpec((1, D_MODEL), lambda i, j: (0, 0)),
            pl.BlockSpec((D_MODEL, COL_TILE), lambda i, j: (0, j)),
            pl.BlockSpec((D_MODEL, SMALL_W), lambda i, j: (0, 0)),
            pl.BlockSpec((ROW_TILE, 128), lambda i, j: (i, 0)),
            pl.BlockSpec((ROW_TILE, 128), lambda i, j: (i, 0)),
        ],
        out_specs=[
            pl.BlockSpec((ROW_TILE, COL_TILE), lambda i, j: (i, j)),
            pl.BlockSpec((ROW_TILE, SMALL_W), lambda i, j: (i, 0)),
        ],
        out_shape=[
            jax.ShapeDtypeStruct((rows, Z_MAIN), F32),
            jax.ShapeDtypeStruct((rows, SMALL_W), F32),
        ],
        scratch_shapes=[pltpu.VMEM((ROW_TILE, D_MODEL), MXU_DTYPE)],
        compiler_params=_cparams(("parallel", "arbitrary")),
        name="inproj",
    )(x_all, gain, w_main, w_small, cos_t, sin_t)


def _rope_tables(pos):
    half = ROPE_DIM // 2
    inv = ROPE_THETA ** (-jnp.arange(half, dtype=F32) / half)
    ang = pos.astype(F32)[:, None] * inv
    cos, sin = jnp.cos(ang), jnp.sin(ang)
    rows = pos.shape[0]
    pad1 = jnp.ones((rows, N_HD - ROPE_DIM), F32)
    pad0 = jnp.zeros((rows, N_HD - ROPE_DIM), F32)
    cos_h = jnp.concatenate([cos, cos, pad1], axis=1)
    sin_h = jnp.concatenate([-sin, sin, pad0], axis=1)
    return jnp.concatenate([cos_h, cos_h], axis=1), jnp.concatenate([sin_h, sin_h], axis=1)


def _mlstm_kernel(zq_ref, zk_ref, zv_ref, zo_ref, zs_ref, hq_ref, hk_ref, cwq_ref, cwk_ref, cbq_ref, cbk_ref,
                  brow_ref, norm_ref, c0_ref, n0_ref, m0_ref,
                  h_ref, c_ref, n_ref, m_ref, c_scr, n_scr, m_scr, hq_scr, hk_scr, *, chunk, nvalid):
    t = pl.program_id(2)
    h = pl.program_id(1)
    rb = zq_ref.shape[0]

    @pl.when(t == 0)
    def _():
        c_scr[...] = c0_ref[0, 0]
        n_scr[...] = n0_ref[0, 0]
        m_scr[...] = m0_ref[0, 0]
        hq_scr[...] = hq_ref[0]
        hk_scr[...] = hk_ref[0]

    def conv_silu(z_ref_, hist_scr, cw_ref, cb_ref):
        z = z_ref_[...]
        ext = jnp.concatenate([hist_scr[...], z], axis=0)
        acc = cb_ref[...]
        for jj in range(M_CONV):
            off = 8 - (M_CONV - 1) + jj
            acc = acc + ext[off:off + rb] * cw_ref[jj:jj + 1, :]
        hist_scr[...] = z[rb - 8:rb]
        return acc * _sigmoid(acc)

    q_all = conv_silu(zq_ref, hq_scr, cwq_ref, cbq_ref)
    k_all = conv_silu(zk_ref, hk_scr, cwk_ref, cbk_ref) * (M_DQK ** -0.5)

    zs = zs_ref[...] + brow_ref[...]
    col = lax.broadcasted_iota(jnp.int32, (rb, SMALL_W), 1)
    li_cols = jnp.sum(jnp.where(col == SM_I + h, zs, 0.0), axis=1, keepdims=True)
    lf_cols = _log_sigmoid(jnp.sum(jnp.where(col == SM_F + h, zs, 0.0), axis=1, keepdims=True))

    L = chunk
    ti = lax.broadcasted_iota(jnp.int32, (L, L), 0)
    si = lax.broadcasted_iota(jnp.int32, (L, L), 1)
    mask = (si <= ti) & (si < nvalid)
    tril = (si <= ti).astype(F32)
    tcol = lax.broadcasted_iota(jnp.int32, (L, 1), 0)

    C = c_scr[...]
    n = n_scr[...]
    m = m_scr[:, 0:1]
    gain = norm_ref[...]
    for c in range(rb // L):
        sl = slice(c * L, (c + 1) * L)
        q, k, v = q_all[sl], k_all[sl], zv_ref[sl, :]
        li_c = li_cols[sl]
        lf_c = jnp.where(tcol < nvalid, lf_cols[sl], 0.0)
        b_c = jnp.dot(tril, jnp.broadcast_to(lf_c, (L, 128)), precision=HIGHEST, preferred_element_type=F32)[:, 0:1]
        b_r = _col_to_row(b_c)
        li_r = _col_to_row(li_c)
        logw = jnp.where(mask, b_c - b_r + li_r, NEG)
        inter = b_c + m
        m_t = jnp.maximum(inter, jnp.max(logw, axis=1, keepdims=True))
        w = jnp.where(mask, jnp.exp(logw - m_t), 0.0) * _mm_nt(q, k)
        w_inter = jnp.exp(inter - m_t)
        num = _mm(w, v) + w_inter * _mm(q, C)
        den = jnp.sum(w, axis=1, keepdims=True) + w_inter * jnp.sum(q * n, axis=1, keepdims=True)
        hh = num / jnp.maximum(jnp.abs(den), jnp.exp(-m_t))
        b_end = b_c[L - 1:L]
        m_new = m_t[L - 1:L]
        w_end = jnp.where(tcol < nvalid, jnp.exp(b_end - b_c + li_c - m_new), 0.0)
        decay = jnp.exp(b_end + m - m_new)
        kw = w_end * k
        C = decay * C + _mm_tn(kw, v)
        n = decay * n + jnp.sum(kw, axis=0, keepdims=True)
        m = m_new
        hh = hh * lax.rsqrt(jnp.mean(hh * hh, axis=-1, keepdims=True) + EPS) * gain
        h_ref[sl, :] = _sigmoid(zo_ref[sl, :]) * hh

    c_scr[...] = C
    n_scr[...] = n
    m_scr[...] = jnp.broadcast_to(m, m_scr.shape)

    @pl.when(t == pl.num_programs(2) - 1)
    def _():
        c_ref[0, 0] = C
        n_ref[0, 0] = n
        m_ref[0, 0] = jnp.broadcast_to(m, m_scr.shape)


def _mlstm(z, zs, row0, nb, rows_per_seq, rb, chunk, nvalid, hist, conv_w, conv_b, brow, norm, c0, n0, m0):
    assert row0 % rb == 0 and rows_per_seq % rb == 0 and rb % chunk == 0
    nt = rows_per_seq // rb
    rblk = lambda b, t: row0 // rb + b * nt + t
    kern = functools.partial(_mlstm_kernel, chunk=chunk, nvalid=nvalid)
    return pl.pallas_call(
        kern,
        grid=(nb, M_HEADS, nt),
        in_specs=[
            pl.BlockSpec((rb, M_DQK), lambda b, h, t: (rblk(b, t), OFF_MQ // M_DQK + h)),
            pl.BlockSpec((rb, M_DQK), lambda b, h, t: (rblk(b, t), OFF_MK // M_DQK + h)),
            pl.BlockSpec((rb, M_DV), lambda b, h, t: (rblk(b, t), OFF_MV // M_DV + h)),
            pl.BlockSpec((rb, M_DV), lambda b, h, t: (rblk(b, t), OFF_MO // M_DV + h)),
            pl.BlockSpec((rb, SMALL_W), lambda b, h, t: (rblk(b, t), 0)),
            pl.BlockSpec((1, 8, M_DQK), lambda b, h, t: (b, 0, OFF_MQ // M_DQK + h)),
            pl.BlockSpec((1, 8, M_DQK), lambda b, h, t: (b, 0, OFF_MK // M_DQK + h)),
            pl.BlockSpec((M_CONV, M_DQK), lambda b, h, t: (0, h)),
            pl.BlockSpec((M_CONV, M_DQK), lambda b, h, t: (0, M_HEADS + h)),
            pl.BlockSpec((1, M_DQK), lambda b, h, t: (0, h)),
            pl.BlockSpec((1, M_DQK), lambda b, h, t: (0, M_HEADS + h)),
            pl.BlockSpec((1, SMALL_W), lambda b, h, t: (0, 0)),
            pl.BlockSpec((1, M_DV), lambda b, h, t: (0, h)),
            pl.BlockSpec((1, 1, M_DQK, M_DV), lambda b, h, t: (b, h, 0, 0)),
            pl.BlockSpec((1, 1, 1, M_DQK), lambda b, h, t: (b, h, 0, 0)),
            pl.BlockSpec((1, 1, 1, 128), lambda b, h, t: (b, h, 0, 0)),
        ],
        out_specs=[
            pl.BlockSpec((rb, M_DV), lambda b, h, t: (b * nt + t, h)),
            pl.BlockSpec((1, 1, M_DQK, M_DV), lambda b, h, t: (b, h, 0, 0)),
            pl.BlockSpec((1, 1, 1, M_DQK), lambda b, h, t: (b, h, 0, 0)),
            pl.BlockSpec((1, 1, 1, 128), lambda b, h, t: (b, h, 0, 0)),
        ],
        out_shape=[
            jax.ShapeDtypeStruct((nb * rows_per_seq, M_HEADS * M_DV), F32),
            jax.ShapeDtypeStruct((nb, M_HEADS, M_DQK, M_DV), F32),
            jax.ShapeDtypeStruct((nb, M_HEADS, 1, M_DQK), F32),
            jax.ShapeDtypeStruct((nb, M_HEADS, 1, 128), F32),
        ],
        scratch_shapes=[
            pltpu.VMEM((M_DQK, M_DV), F32),
            pltpu.VMEM((1, M_DQK), F32),
            pltpu.VMEM((1, 128), F32),
            pltpu.VMEM((8, M_DQK), F32),
            pltpu.VMEM((8, M_DQK), F32),
        ],
        compiler_params=_cparams(("parallel", "parallel", "arbitrary")),
        name="mlstm",
    )(z, z, z, z, zs, hist, hist, conv_w, conv_w, conv_b, conv_b, brow, norm, c0, n0, m0)


KV_W = N_KV * N_HD
HID_W = N_KV * CMP_HIDDEN
TRANSPOSE_UNROLL = 4
BLOCK_PITCH = CMP_BLOCK + 4


def _gelu(x):
    return 0.5 * x * (1.0 + lax.erf(x * (2.0 ** -0.5)))


def _rope_rows(x, cos, sin):
    lane = lax.broadcasted_iota(jnp.int32, x.shape, 1)
    half = ROPE_DIM // 2
    w = x.shape[1]
    partner = jnp.where(lane % N_HD < half, pltpu.roll(x, w - half, 1), pltpu.roll(x, half, 1))
    reps = w // cos.shape[1]
    return x * jnp.concatenate([cos] * reps, axis=1) + partner * jnp.concatenate([sin] * reps, axis=1)


def _store_heads(o_ref, row0, val):
    for kv in range(N_KV):
        o_ref[0, kv, row0:row0 + val.shape[0], :] = val[:, kv * N_HD:(kv + 1) * N_HD]


def _compress_blocks(lo_ref, hi_ref, nb, w1_ref, b1_ref, w2_ref, pitch=CMP_BLOCK):
    parts = []
    for j in range(CMP_BLOCK):
        parts.append(lo_ref[pl.ds(j, nb, stride=pitch), :].astype(MXU_DTYPE))
        parts.append(hi_ref[pl.ds(j, nb, stride=pitch), :].astype(MXU_DTYPE))
    hid = jnp.dot(jnp.concatenate(parts, axis=1), w1_ref[...], preferred_element_type=F32)
    return _mm(_gelu(hid + b1_ref[...]), w2_ref[...])


def _compress_prompt_kernel(lo_ref, hi_ref, w1_ref, b1_ref, w2_ref, cos_ref, sin_ref, o_ref, *, nb):
    out = _compress_blocks(lo_ref, hi_ref, nb, w1_ref, b1_ref, w2_ref)
    _store_heads(o_ref, 0, _rope_rows(out, cos_ref[0:nb, :], sin_ref[0:nb, :]))


def _compress_prompt(z, B, T, col_off, wts, tabs):
    nb = T // CMP_BLOCK
    cos, sin = tabs
    w1, b1, w2 = wts
    kern = functools.partial(_compress_prompt_kernel, nb=nb)
    c2 = lambda b: (0, 0)
    return pl.pallas_call(
        kern,
        grid=(B,),
        in_specs=[pl.BlockSpec((T, 128), lambda b: (b, col_off // 128)),
                  pl.BlockSpec((T, 128), lambda b: (b, col_off // 128 + 1)),
                  pl.BlockSpec((CMP_BLOCK * KV_W, HID_W), c2),
                  pl.BlockSpec((1, HID_W), c2), pl.BlockSpec((HID_W, KV_W), c2),
                  pl.BlockSpec(cos.shape, c2), pl.BlockSpec(sin.shape, c2)],
        out_specs=pl.BlockSpec((1, N_KV, nb, N_HD), lambda b: (b, 0, 0, 0)),
        out_shape=jax.ShapeDtypeStruct((B, N_KV, nb, N_HD), F32),
        compiler_params=_cparams(("parallel",)),
        name="compress_prompt",
    )(z, z, w1, b1, w2, cos, sin)


def _page_t_copy(pt_ref, b, p, pool_ref, buf_ref, sem):
    return pltpu.make_async_copy(pool_ref.at[pt_ref[b, p]],
                                 buf_ref.at[:, pl.ds(pl.multiple_of(p * PAGE_SIZE, PAGE_SIZE), PAGE_SIZE)], sem)


def _compress_sample_kernel(pt_ref, poolk_ref, poolv_ref, new_ref, w1_ref, b1_ref, w2_ref, cos_ref, sin_ref, o_ref,
                            stage2, lo_buf, hi_buf, sem, *, n_pages, nb, n_new):
    c = pl.program_id(0)
    b = pl.program_id(1)
    nseq = pl.num_programs(1)
    step = c * nseq + b
    slot = step % 2

    def gather(which, seq, sl, go):
        pool_ref = poolk_ref if which == 0 else poolv_ref

        def one(p, carry):
            cp = _page_t_copy(pt_ref, seq, p, pool_ref, stage2.at[sl], sem.at[sl])
            if go:
                cp.start()
            else:
                cp.wait()
            return carry
        lax.fori_loop(0, n_pages, one, 0)

    def gather_step(s, sl, go):
        @pl.when(s < nseq)
        def _():
            gather(0, s, sl, go)

        @pl.when(s >= nseq)
        def _():
            gather(1, s - nseq, sl, go)

    @pl.when(step == 0)
    def _():
        gather_step(step, slot, True)

    @pl.when(step + 1 < 2 * nseq)
    def _():
        gather_step(step + 1, 1 - slot, True)

    stage = stage2.at[slot]

    blocks_per_page = PAGE_SIZE // CMP_BLOCK

    def to_rows(p4, c):
        for u in range(TRANSPOSE_UNROLL):
            p = p4 * TRANSPOSE_UNROLL + u
            page = stage[:, pl.ds(pl.multiple_of(p * PAGE_SIZE, PAGE_SIZE), PAGE_SIZE)]
            base = pl.multiple_of(p * (blocks_per_page * BLOCK_PITCH), 8)
            for half, buf in ((0, lo_buf), (1, hi_buf)):
                rows = page[half * 128:(half + 1) * 128, :].T
                for bl in range(blocks_per_page):
                    buf[pl.ds(base + bl * BLOCK_PITCH, CMP_BLOCK), :] = rows[bl * CMP_BLOCK:(bl + 1) * CMP_BLOCK]
        return c

    w1, b1, w2 = w1_ref.at[0], b1_ref.at[0], w2_ref.at[0]
    cos, sin, out_ref = cos_ref.at[0], sin_ref.at[0], o_ref.at[0, 0]
    out_ref[...] = jnp.zeros(out_ref.shape, out_ref.dtype)
    xn = new_ref[...]
    hid = jnp.zeros((8, HID_W), F32)
    for jrow in range(n_new):
        hid = hid + _mm(jnp.broadcast_to(xn[jrow:jrow + 1, :], (8, KV_W)), w1[jrow * KV_W:(jrow + 1) * KV_W, :])
    row = lax.broadcasted_iota(jnp.int32, (8, HID_W), 0)
    hid = jnp.where(row == 0, hid, 0.0) + b1[...]
    new = _mm(_gelu(hid), w2[...])
    out_ref[nb:nb + 8, :] = _rope_rows(new, cos[nb:nb + 8, :], sin[nb:nb + 8, :])
    gather_step(step, slot, False)
    lax.fori_loop(0, n_pages // TRANSPOSE_UNROLL, to_rows, 0)
    out = _compress_blocks(lo_buf, hi_buf, nb, w1, b1, w2, pitch=BLOCK_PITCH)
    out_ref[0:nb, :] = _rope_rows(out, cos[0:nb, :], sin[0:nb, :])


def _compress_sample(page_table, pool_k, pool_v, z, new_row_blk0, new_col_blk, wts_k, wts_v, tabs_k, tabs_v, nbp, n_new):
    BS, n_pages = page_table.shape
    nb = n_pages * PAGE_SIZE // CMP_BLOCK
    stack = lambda a, b: jnp.stack([a, b])
    cos, sin = stack(tabs_k[0], tabs_v[0]), stack(tabs_k[1], tabs_v[1])
    w1, b1, w2 = (stack(a, b) for a, b in zip(wts_k, wts_v))
    kern = functools.partial(_compress_sample_kernel, n_pages=n_pages, nb=nb, n_new=n_new)
    per_pool = lambda shp: pl.BlockSpec((1,) + shp, lambda c, b, pt: (c, 0, 0))
    gs = pltpu.PrefetchScalarGridSpec(
        num_scalar_prefetch=1,
        grid=(2, BS),
        in_specs=[pl.BlockSpec(memory_space=pl.ANY), pl.BlockSpec(memory_space=pl.ANY),
                  pl.BlockSpec((8, KV_W), lambda c, b, pt: (new_row_blk0 + b, new_col_blk + c)),
                  per_pool((CMP_BLOCK * KV_W, HID_W)), per_pool((1, HID_W)), per_pool((HID_W, KV_W)),
                  per_pool((nbp, 128)), per_pool((nbp, 128))],
        out_specs=pl.BlockSpec((1, 1, nbp, KV_W), lambda c, b, pt: (c, b, 0, 0)),
        scratch_shapes=[pltpu.VMEM((2, KV_W, n_pages * PAGE_SIZE), F32),
                        pltpu.VMEM((nb * BLOCK_PITCH, 128), F32), pltpu.VMEM((nb * BLOCK_PITCH, 128), F32),
                        pltpu.SemaphoreType.DMA((2,))],
    )
    return pl.pallas_call(
        kern,
        grid_spec=gs,
        out_shape=jax.ShapeDtypeStruct((2, BS, nbp, KV_W), F32),
        compiler_params=_cparams(("arbitrary", "arbitrary")),
        name="compress_sample",
    )(page_table, pool_k, pool_v, z, w1, b1, w2, cos, sin)


def _compress_weights(w1, b1, w2):
    eye = jnp.eye(N_KV, dtype=w1.dtype)
    w1big = jnp.einsum('jdh,kl->jkdlh', w1, eye).reshape(CMP_BLOCK * KV_W, HID_W)
    w2big = jnp.einsum('hd,kl->khld', w2, eye).reshape(HID_W, KV_W)
    return w1big.astype(MXU_DTYPE), jnp.tile(b1, N_KV).reshape(1, HID_W).astype(F32), w2big.astype(MXU_DTYPE)


def _block_tables(nbp, rope):
    cos, sin = _rope_tables(jnp.arange(nbp, dtype=jnp.int32) * CMP_BLOCK + CMP_BLOCK - 1)
    if not rope:
        cos, sin = jnp.ones_like(cos), jnp.zeros_like(sin)
    return cos, sin


SCALE = N_HD ** -0.5
INF = float('inf')


def _masked_softmax(s, mask):
    s = jnp.where(mask, s, NEG)
    m = jnp.max(s, axis=1, keepdims=True)
    e = jnp.where(mask, jnp.exp(s - m), 0.0)
    return e / jnp.maximum(jnp.sum(e, axis=1, keepdims=True), 1.0)


def _topk_mask(score, k):
    st = score.T
    rid = lax.broadcasted_iota(jnp.int32, st.shape, 0).astype(F32)

    def body(r, carry):
        s, sel = carry
        m = jnp.max(s, axis=0, keepdims=True)
        idx = jnp.min(jnp.where(s == m, rid, float(st.shape[0])), axis=0, keepdims=True)
        hit = rid == idx
        return jnp.where(hit, -INF, s), jnp.where(hit, 1.0, sel)

    _, sel = lax.fori_loop(0, k, body, (st, jnp.zeros(st.shape, F32)))
    return sel.T


def _stack_heads(q, tile):
    return jnp.concatenate([q[:, g * N_HD:(g + 1) * N_HD] for g in range(N_GROUP)], axis=0)


def _compressed_branch(qs, kc, vc, tq, nbc, wp, tile):
    nbp = kc.shape[0]
    blk = lax.broadcasted_iota(jnp.int32, (1, nbp), 1)
    s_c = _mm_nt(qs, kc)
    mask_c = (blk * CMP_BLOCK + CMP_BLOCK - 1 <= tq) & (blk < nbc)
    p_c = _masked_softmax(s_c, mask_c)
    o_cmp = _mm(p_c, vc)
    psum = p_c[0:tile]
    for g in range(1, N_GROUP):
        psum = psum + p_c[g * tile:(g + 1) * tile]
    ratio = SLC_BLOCK // CMP_BLOCK
    pair = jnp.where(lax.broadcasted_iota(jnp.int32, (nbp, wp), 0) // ratio
                     == lax.broadcasted_iota(jnp.int32, (nbp, wp), 1), 1.0, 0.0)
    return o_cmp, jnp.dot(psum, pair, precision=HIGHEST, preferred_element_type=F32)


def _select_blocks(imp, tq_t, n_sel, wp):
    sb = lax.broadcasted_iota(jnp.int32, (1, wp), 1)
    valid = (sb * SLC_BLOCK <= tq_t) & (sb < n_sel)
    forced = (sb == 0) | (sb == tq_t // SLC_BLOCK)
    score = jnp.where(valid, jnp.where(forced, INF, imp), -INF)
    return _topk_mask(score, min(SLC_TOPK, n_sel))


def _expand_sel(sel, kpos, wp):
    sbc = lax.broadcasted_iota(jnp.int32, (wp, 1), 0)
    e = jnp.where(sbc == kpos // SLC_BLOCK, 1.0, 0.0)
    return jnp.dot(sel.astype(jnp.bfloat16), e.astype(jnp.bfloat16), preferred_element_type=F32)


def _online_step(carry, s, mask, v):
    m, l, acc = carry
    s = jnp.where(mask, s, NEG)
    m_new = jnp.maximum(m, jnp.max(s, axis=1, keepdims=True))
    alpha = jnp.exp(m - m_new)
    p = jnp.where(mask, jnp.exp(s - m_new), 0.0)
    return m_new, alpha * l + jnp.sum(p, axis=1, keepdims=True), alpha * acc + _mm(p, v)


def _gate_col(gates, idx):
    col = lax.broadcasted_iota(jnp.int32, gates.shape, 1)
    return jnp.sum(jnp.where(col == idx, gates, 0.0), axis=1, keepdims=True)


def _combine(gates, head0, o_cmp, o_slc, o_win, tile):
    outs = []
    for g in range(N_GROUP):
        rows = slice(g * tile, (g + 1) * tile)
        c0 = SM_G + (head0 + g) * 3
        outs.append(_gate_col(gates, c0) * o_cmp[rows] + _gate_col(gates, c0 + 1) * o_slc[rows]
                    + _gate_col(gates, c0 + 2) * o_win[rows])
    return jnp.concatenate(outs, axis=1)


SLC_TILE = 512
DECODE_CHUNK = 2048
NSA_TILE = 256
WIN_SPAN = WINDOW + NSA_TILE


def _nsa_prompt_kernel(q_ref, zs_ref, kc_ref, vc_ref, ks_ref, vs_ref, kw_ref, vw_ref, o_ref, *, nbc, n_sel, wp):
    kv = pl.program_id(1)
    i = pl.program_id(2)
    tile = NSA_TILE
    qs = _stack_heads(q_ref[...] * SCALE, tile).astype(MXU_DTYPE)
    trow = lax.broadcasted_iota(jnp.int32, (N_GROUP * tile, 1), 0) % tile
    tq = i * tile + trow
    tq_t = i * tile + lax.broadcasted_iota(jnp.int32, (tile, 1), 0)

    o_cmp, imp = _compressed_branch(qs, kc_ref[0, 0], vc_ref[0, 0], tq, nbc, wp, tile)
    sel = _select_blocks(imp, tq_t, n_sel, wp).astype(jnp.bfloat16)

    def one_tile(t, carry):
        m, acc = carry
        k0 = pl.multiple_of(t * SLC_TILE, SLC_TILE)
        kpos = k0 + lax.broadcasted_iota(jnp.int32, (1, SLC_TILE), 1)
        ok = (_expand_sel(sel, kpos, wp) > 0.5) & (kpos <= tq_t)
        bias = jnp.where(ok, 0.0, NEG)
        s = _mm_nt(qs, ks_ref[0, 0, pl.ds(k0, SLC_TILE), :]) + jnp.concatenate([bias] * N_GROUP, axis=0)
        m_new = jnp.maximum(m, jnp.max(s, axis=1, keepdims=True))
        p = jnp.exp(s - m_new)
        return m_new, jnp.exp(m - m_new) * acc + _mm(p, vs_ref[0, 0, pl.ds(k0, SLC_TILE), :])

    def slc_step(j, carry):
        return one_tile(2 * j + 1, one_tile(2 * j, carry))

    rows = N_GROUP * tile
    init = (jnp.full((rows, 1), NEG, F32), jnp.zeros((rows, 2 * N_HD), F32))
    n_pairs = (i * tile + tile - 1) // (2 * SLC_TILE) + 1
    _, acc = lax.fori_loop(0, n_pairs, slc_step, init)
    o_slc = acc[:, :N_HD] / jnp.maximum(acc[:, N_HD:N_HD + 1], 1.0)

    w0 = pl.multiple_of(jnp.maximum(i * tile - WINDOW, 0), tile)
    pw = w0 + lax.broadcasted_iota(jnp.int32, (1, WIN_SPAN), 1)
    dist = tq_t - pw
    bias_w = jnp.where((dist >= 0) & (dist < WINDOW), 0.0, NEG)
    s_w = _mm_nt(qs, kw_ref[0, 0, pl.ds(w0, WIN_SPAN), :]) + jnp.concatenate([bias_w] * N_GROUP, axis=0)
    p_w = jnp.exp(s_w - jnp.max(s_w, axis=1, keepdims=True))
    acc_w = _mm(p_w, vw_ref[0, 0, pl.ds(w0, WIN_SPAN), :])
    o_win = acc_w[:, :N_HD] / jnp.maximum(acc_w[:, N_HD:N_HD + 1], 1.0)

    o_ref[...] = _combine(_sigmoid(zs_ref[...]), kv * N_GROUP, o_cmp, o_slc, o_win, tile)


def _nsa_prompt(z, zs, kc, vc, ks, vs, kw, vw, B, T, nbc, n_sel, wp):
    nq = T // NSA_TILE
    nbp = kc.shape[2]
    kern = functools.partial(_nsa_prompt_kernel, nbc=nbc, n_sel=n_sel, wp=wp)
    seq = lambda b, kv, i: (b, kv, 0, 0)
    return pl.pallas_call(
        kern,
        grid=(B, N_KV, nq),
        in_specs=[pl.BlockSpec((NSA_TILE, KV_W), lambda b, kv, i: (b * nq + i, OFF_NQ // KV_W + kv)),
                  pl.BlockSpec((NSA_TILE, SMALL_W), lambda b, kv, i: (b * nq + i, 0)),
                  pl.BlockSpec((1, 1, nbp, N_HD), seq), pl.BlockSpec((1, 1, nbp, N_HD), seq),
                  pl.BlockSpec((1, 1, T, N_HD), seq), pl.BlockSpec((1, 1, T, 2 * N_HD), seq),
                  pl.BlockSpec((1, 1, T, N_HD), seq), pl.BlockSpec((1, 1, T, 2 * N_HD), seq)],
        out_specs=pl.BlockSpec((NSA_TILE, KV_W), lambda b, kv, i: (b * nq + i, kv)),
        out_shape=jax.ShapeDtypeStruct((B * T, N_HEADS * N_HD), F32),
        compiler_params=_cparams(("parallel", "parallel", "arbitrary")),
        name="nsa_prompt",
    )(z, zs, kc, vc, ks, vs, kw, vw)


def _nsa_sample_kernel(pt_ref, q_ref, zs_ref, kc_ref, vc_ref, ksn_ref, vsn_ref, kwn_ref, vwn_ref, wk_ref, wv_ref,
                       e_hbm, pk_ref, pv_ref, o_ref, kbuf2, vbuf2, sem_k, sem_v, e_ref, sem_e,
                       *, past, n_new, nbc, n_sel, wp, chunk):
    b = pl.program_id(0)
    n_pages = past // PAGE_SIZE
    tile = SAMPLE_PAD
    rows = N_GROUP * tile
    slot = b % 2

    def gather(seq, sl, go):
        def one(p, c):
            ck = _page_t_copy(pt_ref, seq, p, pk_ref, kbuf2.at[sl], sem_k.at[sl])
            cv = _page_t_copy(pt_ref, seq, p, pv_ref, vbuf2.at[sl], sem_v.at[sl])
            if go:
                ck.start()
                cv.start()
            else:
                ck.wait()
                cv.wait()
            return c
        lax.fori_loop(0, n_pages, one, 0)

    @pl.when(b == 0)
    def _():
        gather(b, slot, True)
        load_e = pltpu.make_async_copy(e_hbm, e_ref, sem_e)
        load_e.start()
        load_e.wait()

    @pl.when(b + 1 < pl.num_programs(0))
    def _():
        gather(b + 1, 1 - slot, True)

    kbuf, vbuf = kbuf2.at[slot], vbuf2.at[slot]

    q8 = q_ref[...] * SCALE
    lane = lax.broadcasted_iota(jnp.int32, (tile, KV_W), 1)
    qrows = []
    for kv in range(N_KV):
        sec = q8[:, kv * KV_W:(kv + 1) * KV_W]
        for g in range(N_GROUP):
            moved = sec if g == kv else pltpu.roll(sec, ((kv - g) % N_KV) * N_HD, 1)
            qrows.append(jnp.where(lane // N_HD == kv, moved, 0.0))
    qbd = jnp.concatenate(qrows, axis=0).astype(MXU_DTYPE)

    def own_head(full):
        return jnp.concatenate([full[kv * rows:(kv + 1) * rows, kv * N_HD:(kv + 1) * N_HD] for kv in range(N_KV)], axis=0)

    nrow = N_KV * rows
    tq = past + lax.broadcasted_iota(jnp.int32, (nrow, 1), 0) % tile
    newpos = past + lax.broadcasted_iota(jnp.int32, (1, tile), 1)
    new_ok = (newpos < past + n_new) & (newpos <= tq)

    kc, vc = kc_ref[0], vc_ref[0]
    nbp = kc.shape[0]
    blk = lax.broadcasted_iota(jnp.int32, (1, nbp), 1)
    p_c = _masked_softmax(_mm_nt(qbd, kc), (blk * CMP_BLOCK + CMP_BLOCK - 1 <= tq) & (blk < nbc))
    o_cmp = own_head(_mm(p_c, vc))
    psum = []
    for kv in range(N_KV):
        acc = p_c[kv * rows:kv * rows + tile]
        for g in range(1, N_GROUP):
            acc = acc + p_c[kv * rows + g * tile:kv * rows + (g + 1) * tile]
        psum.extend([acc] * N_GROUP)
    ratio = SLC_BLOCK // CMP_BLOCK
    pair = jnp.where(lax.broadcasted_iota(jnp.int32, (nbp, wp), 0) // ratio
                     == lax.broadcasted_iota(jnp.int32, (nbp, wp), 1), 1.0, 0.0)
    imp = jnp.dot(jnp.concatenate(psum, axis=0), pair, precision=HIGHEST, preferred_element_type=F32)
    sel_rows = _select_blocks(imp, tq, n_sel, wp).astype(jnp.bfloat16)

    wpos = past - WINDOW + lax.broadcasted_iota(jnp.int32, (1, WINDOW), 1)
    wdist = tq - wpos
    win_ok = (wdist >= 0) & (wdist < WINDOW) & (wpos >= 0)
    s1 = jnp.where(win_ok, _mm(qbd, wk_ref[0]), NEG)
    s2 = jnp.where(new_ok, _mm_nt(qbd, kwn_ref[...]), NEG)
    m = jnp.maximum(jnp.max(s1, axis=1, keepdims=True), jnp.max(s2, axis=1, keepdims=True))
    e1 = jnp.where(win_ok, jnp.exp(s1 - m), 0.0)
    e2 = jnp.where(new_ok, jnp.exp(s2 - m), 0.0)
    den = jnp.maximum(jnp.sum(e1, axis=1, keepdims=True) + jnp.sum(e2, axis=1, keepdims=True), 1.0)
    o_win = own_head(_mm_nt(e1, wv_ref[0]) + _mm(e2, vwn_ref[...])) / den

    gather(b, slot, False)

    carry = (jnp.full((nrow, 1), NEG, F32), jnp.zeros((nrow, 1), F32), jnp.zeros((nrow, KV_W), F32))
    for c in range(past // chunk):
        cs = slice(c * chunk, (c + 1) * chunk)
        s = jnp.dot(qbd, kbuf[:, cs].astype(MXU_DTYPE), preferred_element_type=F32)
        hit = jnp.dot(sel_rows, e_ref[:, cs], preferred_element_type=F32) > 0.5
        mm, l, acc = carry
        s = jnp.where(hit, s, NEG)
        m_new = jnp.maximum(mm, jnp.max(s, axis=1, keepdims=True))
        alpha = jnp.exp(mm - m_new)
        p = jnp.where(hit, jnp.exp(s - m_new), 0.0)
        carry = (m_new, alpha * l + jnp.sum(p, axis=1, keepdims=True), alpha * acc + _mm_nt(p, vbuf[:, cs]))
    hit_new = (_expand_sel(sel_rows, newpos, wp) > 0.5) & new_ok
    _, l, acc = _online_step(carry, _mm_nt(qbd, ksn_ref[...]), hit_new, vsn_ref[...])
    o_slc = own_head(acc) / jnp.maximum(l, 1.0)

    gates = _sigmoid(zs_ref[...])
    gcol = lambda c: jnp.concatenate([gates[:, SM_G + hh * 3 + c:SM_G + hh * 3 + c + 1] for hh in range(N_HEADS)], axis=0)
    comb = gcol(0) * o_cmp + gcol(1) * o_slc + gcol(2) * o_win
    o_ref[...] = jnp.concatenate([comb[hh * tile:(hh + 1) * tile] for hh in range(N_HEADS)], axis=1)


def _nsa_sample(page_table, z, zs, row_blk0, kc, vc, win_k, win_v, pool_k, pool_v, n_new, nbc, n_sel, wp):
    BS, n_pages = page_table.shape
    past = n_pages * PAGE_SIZE
    chunk = min(past, DECODE_CHUNK)
    kern = functools.partial(_nsa_sample_kernel, past=past, n_new=n_new, nbc=nbc, n_sel=n_sel, wp=wp, chunk=chunk)
    new = lambda off: pl.BlockSpec((SAMPLE_PAD, KV_W), lambda b, pt: (row_blk0 + b, off // KV_W))
    seq3 = lambda b, pt: (b, 0, 0)
    expand = (jnp.arange(wp, dtype=jnp.int32)[:, None] == jnp.arange(past, dtype=jnp.int32)[None, :] // SLC_BLOCK
              ).astype(jnp.bfloat16)
    gs = pltpu.PrefetchScalarGridSpec(
        num_scalar_prefetch=1,
        grid=(BS,),
        in_specs=[pl.BlockSpec((SAMPLE_PAD, N_HEADS * N_HD), lambda b, pt: (row_blk0 + b, OFF_NQ // (N_HEADS * N_HD))),
                  pl.BlockSpec((SAMPLE_PAD, SMALL_W), lambda b, pt: (row_blk0 + b, 0)),
                  pl.BlockSpec((1, kc.shape[1], KV_W), seq3), pl.BlockSpec((1, kc.shape[1], KV_W), seq3),
                  new(OFF_KS), new(OFF_VS), new(OFF_KW), new(OFF_VW),
                  pl.BlockSpec((1, KV_W, WINDOW), seq3), pl.BlockSpec((1, KV_W, WINDOW), seq3),
                  pl.BlockSpec(memory_space=pl.ANY),
                  pl.BlockSpec(memory_space=pl.ANY), pl.BlockSpec(memory_space=pl.ANY)],
        out_specs=pl.BlockSpec((SAMPLE_PAD, N_HEADS * N_HD), lambda b, pt: (b, 0)),
        scratch_shapes=[pltpu.VMEM((2, KV_W, past), F32), pltpu.VMEM((2, KV_W, past), F32),
                        pltpu.SemaphoreType.DMA((2,)), pltpu.SemaphoreType.DMA((2,)),
                        pltpu.VMEM((wp, past), jnp.bfloat16), pltpu.SemaphoreType.DMA(())],
    )
    return pl.pallas_call(
        kern,
        grid_spec=gs,
        out_shape=jax.ShapeDtypeStruct((BS * SAMPLE_PAD, N_HEADS * N_HD), F32),
        compiler_params=_cparams(("arbitrary",)),
        name="nsa_sample",
    )(page_table, z, zs, kc, vc, z, z, z, z, win_k, win_v, expand, pool_k, pool_v)


MERGE_TILE = 256
P_QW = P_HEADS * P_DKEY
P_EXPERTS = P_NKEYS * P_NKEYS
P_HK = P_HEADS * P_TOPK


def _merge_kernel(x_ref, ha_ref, hb_ref, ga0_ref, ga1_ref, gb0_ref, gb1_ref, wo_ref, gain_ref, wq_ref,
                  x1_ref, xn_ref, qp_ref):
    ga = _sigmoid(jnp.concatenate([ga0_ref[...], ga1_ref[...]], axis=1))
    gb = _sigmoid(jnp.concatenate([gb0_ref[...], gb1_ref[...]], axis=1))
    x1 = x_ref[...] + _mm(ga * ha_ref[...] + gb * hb_ref[...], wo_ref[...])
    x1_ref[...] = x1
    xn = (x1 * lax.rsqrt(jnp.mean(x1 * x1, axis=-1, keepdims=True) + EPS) * gain_ref[...]).astype(MXU_DTYPE)
    xn_ref[...] = xn
    qp_ref[...] = jnp.dot(xn, wq_ref[...], preferred_element_type=F32)


def _merge(x_all, z, ha, hb, row0, w_out, gain, wq):
    n = ha.shape[0]
    tm = MERGE_TILE
    assert n % tm == 0 and row0 % tm == 0
    r0 = row0 // tm
    zc = lambda off: pl.BlockSpec((tm, COL_TILE), lambda i: (r0 + i, off // COL_TILE))
    full = lambda shp: pl.BlockSpec(shp, lambda i: (0, 0))
    loc = lambda w: pl.BlockSpec((tm, w), lambda i: (i, 0))
    return pl.pallas_call(
        _merge_kernel,
        grid=(n // tm,),
        in_specs=[pl.BlockSpec((tm, D_MODEL), lambda i: (r0 + i, 0)), loc(D_MODEL), loc(D_MODEL),
                  zc(OFF_GA), zc(OFF_GA + COL_TILE), zc(OFF_GB), zc(OFF_GB + COL_TILE),
                  full((D_MODEL, D_MODEL)), full((1, D_MODEL)), full((D_MODEL, P_QW))],
        out_specs=[loc(D_MODEL), loc(D_MODEL), loc(P_QW)],
        out_shape=[jax.ShapeDtypeStruct((n, D_MODEL), F32), jax.ShapeDtypeStruct((n, D_MODEL), MXU_DTYPE),
                   jax.ShapeDtypeStruct((n, P_QW), F32)],
        compiler_params=_cparams(("parallel",)),
        name="merge",
    )(x_all, ha, hb, z, z, z, z, w_out, gain, wq)


PEER_TILE = 256
PEER_EB = 2048
PEER_SLABS = PEER_EB // P_NKEYS
W_PITCH = P_NKEYS + 4


def _pop_topk_cols(scores, k, emits):
    def body(r, ss):
        out = []
        for s, emit in zip(ss, emits):
            rid = lax.broadcasted_iota(jnp.int32, s.shape, 0).astype(F32)
            m = jnp.max(s, axis=0, keepdims=True)
            idx = jnp.min(jnp.where(s == m, rid, float(s.shape[0])), axis=0, keepdims=True)
            hit = rid == idx
            emit(r, m, idx, hit)
            out.append(jnp.where(hit, -INF, s))
        return tuple(out)

    lax.fori_loop(0, k, body, tuple(scores))


_CAND_GROUPS = [(0, 0), (0, 8)] + [(r0, 0) for r0 in range(1, P_TOPK)]


def _peer_select(qp_ref, sk_ref, a_scr, b_scr, g_scr, v_scr, i_scr, e_scr, bv_scr):
    tn = qp_ref.shape[0]
    sub8 = lax.broadcasted_iota(jnp.int32, (8, tn), 0)
    for h in range(P_HEADS):
        scores, emits = [], []
        for c in range(2):
            hc = 2 * h + c
            scores.append(_mm_nt(sk_ref[hc], qp_ref[:, hc * P_NKEYS:(hc + 1) * P_NKEYS]))

            def emit(r, m, idx, hit, hc=hc):
                v_scr[pl.ds(hc * P_TOPK + r, 1), :] = m
                i_scr[pl.ds(hc * P_TOPK + r, 1), :] = idx

            emits.append(emit)
        _pop_topk_cols(scores, P_TOPK, emits)
    for h0 in range(0, P_HEADS, 2):
        cands, emits = [], []
        for h in (h0, h0 + 1):
            o0, o1 = 2 * h * P_TOPK, (2 * h + 1) * P_TOPK
            cand, cexp = [], []
            for r0, r1 in _CAND_GROUPS:
                ok = (r0 + 1) * (r1 + sub8 + 1) <= P_TOPK
                cand.append(jnp.where(ok, v_scr[o0 + r0:o0 + r0 + 1, :] + v_scr[o1 + r1:o1 + r1 + 8, :], -INF))
                cexp.append(i_scr[o0 + r0:o0 + r0 + 1, :] * float(P_NKEYS) + i_scr[o1 + r1:o1 + r1 + 8, :])
            cands.append(jnp.concatenate(cand, axis=0))
            cexp = jnp.concatenate(cexp, axis=0)

            def emit2(r, m, idx, hit, h=h, cexp=cexp):
                e_scr[pl.ds(h * P_TOPK + r, 1), :] = jnp.sum(jnp.where(hit, cexp, 0.0), axis=0, keepdims=True)
                bv_scr[pl.ds(h * P_TOPK + r, 1), :] = m

            emits.append(emit2)
        _pop_topk_cols(cands, P_TOPK, emits)
    gates = []
    for h in range(P_HEADS):
        best = bv_scr[h * P_TOPK:(h + 1) * P_TOPK, :]
        ex = jnp.exp(best - best[0:1, :])
        gates.append(ex / jnp.sum(ex, axis=0, keepdims=True))
    e = e_scr[...]
    a = jnp.floor(e * (1.0 / P_NKEYS))
    a_scr[...] = a.T
    b_scr[...] = (e - a * float(P_NKEYS)).T
    g_scr[...] = jnp.concatenate(gates, axis=0).T


def _peer_kernel(x1_ref, xn_ref, qp_ref, sk_ref, u_ref, v_ref, gain_ref, y_ref,
                 a_scr, b_scr, g_scr, w_scr, acc_scr, v_scr, i_scr, e_scr, bv_scr):
    j = pl.program_id(1)
    tn = x1_ref.shape[0]

    @pl.when(j == 0)
    def _():
        _peer_select(qp_ref, sk_ref, a_scr, b_scr, g_scr, v_scr, i_scr, e_scr, bv_scr)
        sub = lax.broadcasted_iota(jnp.int32, (P_NKEYS, P_HK), 0).astype(F32)

        def per_group(n8, c):
            r0 = pl.multiple_of(n8 * 8, 8)
            a8, b8, g8 = a_scr[pl.ds(r0, 8), :], b_scr[pl.ds(r0, 8), :], g_scr[pl.ds(r0, 8), :]
            for t in range(8):
                at = jnp.where(sub == a8[t:t + 1], g8[t:t + 1], 0.0)
                bt = jnp.where(sub == b8[t:t + 1], 1.0, 0.0)
                w_scr[pl.ds(pl.multiple_of(r0 * W_PITCH, 8) + t * W_PITCH, P_NKEYS), :] = _mm_nt(at, bt)
            return c

        lax.fori_loop(0, tn // 8, per_group, 0)
        acc_scr[...] = jnp.zeros(acc_scr.shape, F32)

    act = _gelu(_mm_nt(xn_ref[...], u_ref[...]))
    parts = []
    for r in range(PEER_SLABS):
        w = w_scr[pl.ds(j * PEER_SLABS + r, tn, stride=W_PITCH), :]
        parts.append(w * act[:, r * P_NKEYS:(r + 1) * P_NKEYS])
    acc_scr[...] += _mm(jnp.concatenate(parts, axis=1), v_ref[...])

    @pl.when(j == pl.num_programs(1) - 1)
    def _():
        x2 = x1_ref[...] + acc_scr[...]
        y_ref[...] = x2 * lax.rsqrt(jnp.mean(x2 * x2, axis=-1, keepdims=True) + EPS) * gain_ref[...]


def _peer(x1, xn, qp, subkeys, u, v, gain):
    n = x1.shape[0]
    tn = PEER_TILE
    assert n % tn == 0
    tok = lambda w: pl.BlockSpec((tn, w), lambda i, j: (i, 0))
    return pl.pallas_call(
        _peer_kernel,
        grid=(n // tn, P_EXPERTS // PEER_EB),
        in_specs=[tok(D_MODEL), tok(D_MODEL), tok(P_QW),
                  pl.BlockSpec((2 * P_HEADS, P_NKEYS, P_DKEY // 2), lambda i, j: (0, 0, 0)),
                  pl.BlockSpec((PEER_EB, D_MODEL), lambda i, j: (j, 0)),
                  pl.BlockSpec((PEER_EB, D_MODEL), lambda i, j: (j, 0)),
                  pl.BlockSpec((1, D_MODEL), lambda i, j: (0, 0))],
        out_specs=tok(D_MODEL),
        out_shape=jax.ShapeDtypeStruct((n, D_MODEL), F32),
        scratch_shapes=[pltpu.VMEM((tn, P_HK), F32), pltpu.VMEM((tn, P_HK), F32), pltpu.VMEM((tn, P_HK), F32),
                        pltpu.VMEM((tn * W_PITCH, P_NKEYS), F32), pltpu.VMEM((tn, D_MODEL), F32),
                        pltpu.VMEM((2 * P_HEADS * P_TOPK, tn), F32), pltpu.VMEM((2 * P_HEADS * P_TOPK, tn), F32),
                        pltpu.VMEM((P_HK, tn), F32), pltpu.VMEM((P_HK, tn), F32)],
        compiler_params=_cparams(("parallel", "arbitrary")),
        name="peer",
    )(x1, xn, qp, subkeys, u, v, gain)


def _split_w_in(w):
    widths = [2 * M_HEADS * M_DQK, M_HEADS * M_DV, M_HEADS, M_HEADS, M_HEADS * M_DV, N_HEADS * N_HD] + \
             [N_KV * N_HD] * 6 + [3 * N_HEADS, D_MODEL, D_MODEL]
    names = ['m_qk', 'm_v', 'm_i', 'm_f', 'm_o', 'n_q', 'n_kc', 'n_vc', 'n_ks', 'n_vs', 'n_kw', 'n_vw', 'n_g', 'g_a', 'g_b']
    out, off = {}, 0
    for nme, wd in zip(names, widths):
        out[nme] = w[:, off:off + wd]
        off += wd
    return out


def _stage_a(x_prompt, x_sample, past, norm_mix, w_in):
    B, T, D = x_prompt.shape
    BS, TS, _ = x_sample.shape
    xs_pad = jnp.pad(x_sample, ((0, 0), (0, SAMPLE_PAD - TS), (0, 0)))
    x_all = jnp.concatenate([x_prompt.reshape(B * T, D), xs_pad.reshape(BS * SAMPLE_PAD, D)], axis=0)
    rows = x_all.shape[0]
    rows_pad = -(-rows // ROW_TILE) * ROW_TILE
    x_all = jnp.pad(x_all, ((0, rows_pad - rows), (0, 0)))
    pos = jnp.concatenate([jnp.tile(jnp.arange(T, dtype=jnp.int32), B),
                           jnp.tile(past + jnp.arange(SAMPLE_PAD, dtype=jnp.int32), BS),
                           jnp.zeros((rows_pad - rows,), jnp.int32)])
    cos_t, sin_t = _rope_tables(pos)
    cols = _split_w_in(w_in)
    w_main = jnp.concatenate([cols[k] for k in ('m_qk', 'm_v', 'm_o', 'n_q', 'n_kc', 'n_vc', 'n_ks', 'n_vs',
                                                'n_kw', 'n_vw', 'g_a', 'g_b')], axis=1).astype(MXU_DTYPE)
    w_small = jnp.concatenate([cols['m_i'], cols['m_f'], cols['n_g'],
                               jnp.zeros((D, SMALL_W - 2 * M_HEADS - 3 * N_HEADS), w_in.dtype)], axis=1).astype(MXU_DTYPE)
    z, zs = _inproj(x_all, norm_mix.reshape(1, D), w_main, w_small, cos_t, sin_t)
    return x_all, z, zs


def kernel(x_prompt, x_sample, cache_cmp_k, cache_cmp_v, cache_slc_k, cache_slc_v, state_win_k, state_win_v, state_conv, state_C, state_n, state_m, page_table, norm_mix, w_in, m_conv_w, m_conv_b, m_bias_i, m_bias_f, m_norm, cmp_k_w1, cmp_k_b1, cmp_k_w2, cmp_v_w1, cmp_v_b1, cmp_v_w2, w_out, norm_ffn, peer_wq, peer_subkeys, peer_u, peer_v, norm_final):
    B, T, D = x_prompt.shape
    BS, TS, _ = x_sample.shape
    past = page_table.shape[1] * PAGE_SIZE
    assert TS <= SAMPLE_PAD and past >= WINDOW and T >= WIN_SPAN
    assert T % MLSTM_ROWS == 0 and T % (2 * SLC_TILE) == 0 and T % NSA_TILE == 0
    assert (past // PAGE_SIZE) % TRANSPOSE_UNROLL == 0
    n_rows_p, n_rows_s = B * T, BS * SAMPLE_PAD
    x_all, z, zs = _stage_a(x_prompt, x_sample, past, norm_mix[0], w_in[0])
    zeros = lambda *s: jnp.zeros(s, F32)

    brow = jnp.concatenate([m_bias_i[0], m_bias_f[0], zeros(SMALL_W - 2 * M_HEADS)]).reshape(1, -1)
    norm = m_norm[0].reshape(1, -1)
    cb = m_conv_b[0].reshape(1, -1)
    qk_w = 2 * M_HEADS * M_DQK
    ha_p, C_p, n_p, m_p = _mlstm(z, zs, 0, B, T, MLSTM_ROWS, MLSTM_CHUNK, MLSTM_CHUNK, zeros(B, 8, qk_w), m_conv_w[0], cb, brow,
                                 norm, zeros(B, M_HEADS, M_DQK, M_DV), zeros(B, M_HEADS, 1, M_DQK),
                                 zeros(B, M_HEADS, 1, 128))
    hist = jnp.pad(state_conv[0], ((0, 0), (8 - (M_CONV - 1), 0), (0, 0)))
    m0 = jnp.broadcast_to(state_m[0][:, :, None, None], (BS, M_HEADS, 1, 128))
    ha_s, C_s, n_s, m_s = _mlstm(z, zs, n_rows_p, BS, SAMPLE_PAD, SAMPLE_PAD, SAMPLE_PAD, TS, hist, m_conv_w[0], cb,
                                 brow, norm, state_C[0], state_n[0][:, :, None, :], m0)

    sect = lambda off, r0, r1: z[r0:r1, off:off + KV_W]
    wk = _compress_weights(cmp_k_w1[0], cmp_k_b1[0], cmp_k_w2[0])
    wv = _compress_weights(cmp_v_w1[0], cmp_v_b1[0], cmp_v_w2[0])
    up128 = lambda n: -(-n // 128) * 128
    nbc_p = T // CMP_BLOCK
    n_sel_p = T // SLC_BLOCK
    kcmp_p, vcmp_p = sect(OFF_KC, 0, n_rows_p), sect(OFF_VC, 0, n_rows_p)
    kc_p = _compress_prompt(z, B, T, OFF_KC, wk, _block_tables(nbc_p, True))
    vc_p = _compress_prompt(z, B, T, OFF_VC, wv, _block_tables(nbc_p, False))
    prompt_sec = {off: sect(off, 0, n_rows_p) for off in (OFF_KS, OFF_VS, OFF_KW, OFF_VW)}
    heads_first = lambda a: a.reshape(B, T, N_KV, N_HD).transpose(0, 2, 1, 3).astype(MXU_DTYPE)
    ks_h, vs_h, kw_h, vw_h = [heads_first(prompt_sec[o]) for o in (OFF_KS, OFF_VS, OFF_KW, OFF_VW)]
    ones_col = (jnp.arange(N_HD) == 0).astype(MXU_DTYPE)
    aug = lambda v: jnp.concatenate([v, jnp.broadcast_to(ones_col, v.shape)], axis=-1)
    hb_p = _nsa_prompt(z, zs, kc_p, vc_p, ks_h, aug(vs_h), kw_h, aug(vw_h), B, T, nbc_p, n_sel_p, up128(n_sel_p))

    nb_past = past // CMP_BLOCK
    nbc_s = nb_past + SLC_BLOCK // CMP_BLOCK
    n_sel_s = nbc_s * CMP_BLOCK // SLC_BLOCK
    nbp_s = up128(nb_past + 8)
    row_blk_s = n_rows_p // SAMPLE_PAD
    pool_t = lambda pool: pool[0].transpose(0, 2, 3, 1).reshape(-1, KV_W, PAGE_SIZE)
    assert OFF_VC == OFF_KC + KV_W
    kvc_s = _compress_sample(page_table, pool_t(cache_cmp_k), pool_t(cache_cmp_v), z, row_blk_s, OFF_KC // KV_W, wk, wv,
                             _block_tables(nbp_s, True), _block_tables(nbp_s, False), nbp_s, TS)
    kc_s, vc_s = kvc_s[0], kvc_s[1]
    n_win = state_win_k.shape[2]
    assert n_win == WINDOW
    win_t = lambda w: w[0].transpose(0, 2, 3, 1).reshape(BS, KV_W, n_win)
    hb_s = _nsa_sample(page_table, z, zs, row_blk_s, kc_s, vc_s, win_t(state_win_k), win_t(state_win_v),
                       pool_t(cache_slc_k), pool_t(cache_slc_v),
                       TS, nbc_s, n_sel_s, up128(n_sel_s))

    wo = w_out[0].astype(MXU_DTYPE)
    wq = peer_wq[0].astype(MXU_DTYPE)
    gain_ffn = norm_ffn[0].reshape(1, D)
    sk = peer_subkeys[0].reshape(2 * P_HEADS, P_NKEYS, P_DKEY // 2).astype(MXU_DTYPE)
    pu, pv = peer_u[0].astype(MXU_DTYPE), peer_v[0].astype(MXU_DTYPE)
    gain_out = norm_final.reshape(1, D)
    ys = []
    for ha, hb, row0 in ((ha_p, hb_p, 0), (ha_s, hb_s, n_rows_p)):
        x1, xn, qp = _merge(x_all, z, ha, hb, row0, wo, gain_ffn, wq)
        ys.append(_peer(x1, xn, qp, sk, pu, pv, gain_out))
    y_prompt = ys[0].reshape(B, T, D)
    y_sample = ys[1].reshape(BS, SAMPLE_PAD, D)[:, :TS]

    kvshape_p = (1, B, T, N_KV, N_HD)
    zsam = z[n_rows_p:n_rows_p + n_rows_s].reshape(BS, SAMPLE_PAD, Z_MAIN)[:, :TS]
    new_s = lambda off: zsam[:, :, off:off + KV_W].reshape(1, BS, TS, N_KV, N_HD)
    n_win_p = min(WINDOW, T)
    win_p = lambda off: prompt_sec[off].reshape(kvshape_p)[:, :, T - n_win_p:]
    win_s = lambda state, off: jnp.concatenate([state, new_s(off)], axis=2)[:, :, -n_win:]
    conv_p = z[:n_rows_p, OFF_MQ:OFF_MQ + qk_w].reshape(1, B, T, qk_w)[:, :, T - (M_CONV - 1):]
    conv_s = jnp.concatenate([state_conv, zsam[None, :, :, OFF_MQ:OFF_MQ + qk_w]], axis=2)[:, :, -(M_CONV - 1):]
    return (y_prompt, y_sample,
            kcmp_p.reshape(kvshape_p), new_s(OFF_KC), vcmp_p.reshape(kvshape_p), new_s(OFF_VC),
            prompt_sec[OFF_KS].reshape(kvshape_p), new_s(OFF_KS), prompt_sec[OFF_VS].reshape(kvshape_p), new_s(OFF_VS),
            win_p(OFF_KW), win_s(state_win_k, OFF_KW), win_p(OFF_VW), win_s(state_win_v, OFF_VW),
            conv_p, conv_s, C_p[None], C_s[None], n_p[None, :, :, 0], n_s[None, :, :, 0],
            m_p[None, :, :, 0, 0], m_s[None, :, :, 0, 0])
```
